```python
import math
import jax, jax.numpy as jnp
from jax import lax
import numpy as np


D_MODEL = 1024
BATCH = 8
SEQ = 2048
DEPTH = 1
DEC_BATCH = 16
DEC_SEQ = 64
PAST_LEN = 4096

CHUNK = 64
PLE_DIM = 256
SB_WIDTH = D_MODEL // 2
SB_HEAD_DIM = 64
SB_HEADS = SB_WIDTH // SB_HEAD_DIM
SB_BLOCK = 128
SSM_WIDTH = D_MODEL // 2
SSM_GROUP = 16
SSM_GROUPS = SSM_WIDTH // SSM_GROUP
SSM_STATE = 64
D_FF = -(-8 * D_MODEL // (3 * 256)) * 256
IN_WIDTH = 3 * SB_WIDTH + SSM_WIDTH + 2 * D_MODEL
RMS_EPS = 1e-6

kernel_name = 'stickbreak_s5_streaming_step'


def _rmsnorm(x, gain):
    xf = x.astype(jnp.float32)
    y = xf * lax.rsqrt(jnp.mean(xf * xf, axis=-1, keepdims=True) + RMS_EPS)
    return (y * gain.astype(jnp.float32)).astype(x.dtype)


def _sb_block(qb, start, k, v, q_offset):
    z = jnp.einsum('bqhd,bkhd->bhqk', qb, k) * (SB_HEAD_DIM ** -0.5)
    t_pos = q_offset + start + jnp.arange(qb.shape[1])
    s_pos = jnp.arange(k.shape[1])
    visible = s_pos[None, :] < t_pos[:, None]
    log_keep = jnp.where(visible, jax.nn.log_sigmoid(-z), 0.0)
    after = lax.cumsum(log_keep, axis=3, reverse=True) - log_keep
    w = jnp.where(visible, jnp.exp(jax.nn.log_sigmoid(z) + after), 0.0)
    return jnp.einsum('bhqk,bkhd->bqhd', w, v)


def _stick_breaking(q, k, v, q_offset):
    bsz, lq = q.shape[0], q.shape[1]
    blk = min(SB_BLOCK, lq)
    nb = lq // blk
    qf = q.astype(jnp.float32).reshape(bsz, nb, blk, SB_HEADS, SB_HEAD_DIM)
    qf = jnp.moveaxis(qf, 1, 0)
    kf = k.astype(jnp.float32)
    vf = v.astype(jnp.float32)
    starts = jnp.arange(nb) * blk
    out = lax.map(lambda a: _sb_block(a[0], a[1], kf, vf, q_offset), (qf, starts))
    return jnp.moveaxis(out, 0, 1).reshape(bsz, lq, SB_HEADS, SB_HEAD_DIM)


def _s5(u, s_re0, s_im0, a_re, a_im, log_dt, b_re, b_im, c_re, c_im, d, w_glu):
    bsz, L, _ = u.shape
    uf = u.astype(jnp.float32).reshape(bsz, L, SSM_GROUPS, SSM_GROUP)
    a_re = a_re.astype(jnp.float32)
    a_im = a_im.astype(jnp.float32)
    dt = jnp.exp(log_dt.astype(jnp.float32))[:, None]
    mag = jnp.exp(a_re * dt)
    lb_re = mag * jnp.cos(a_im * dt)
    lb_im = mag * jnp.sin(a_im * dt)
    den = a_re * a_re + a_im * a_im
    nr, ni = lb_re - 1.0, lb_im
    f_re = (nr * a_re + ni * a_im) / den
    f_im = (ni * a_re - nr * a_im) / den
    b_re = b_re.astype(jnp.float32)
    b_im = b_im.astype(jnp.float32)
    bb_re = f_re[:, :, None] * b_re - f_im[:, :, None] * b_im
    bb_im = f_re[:, :, None] * b_im + f_im[:, :, None] * b_re
    bu_re = jnp.einsum('gpc,btgc->btgp', bb_re, uf)
    bu_im = jnp.einsum('gpc,btgc->btgp', bb_im, uf)
    s_re0 = s_re0.astype(jnp.float32)
    s_im0 = s_im0.astype(jnp.float32)
    bu_re = bu_re.at[:, 0].add(lb_re * s_re0 - lb_im * s_im0)
    bu_im = bu_im.at[:, 0].add(lb_re * s_im0 + lb_im * s_re0)
    ar = jnp.broadcast_to(lb_re, bu_re.shape)
    ai = jnp.broadcast_to(lb_im, bu_im.shape)

    def combine(e1, e2):
        ar1, ai1, br1, bi1 = e1
        ar2, ai2, br2, bi2 = e2
        return (ar2 * ar1 - ai2 * ai1,
                ar2 * ai1 + ai2 * ar1,
                ar2 * br1 - ai2 * bi1 + br2,
                ar2 * bi1 + ai2 * br1 + bi2)

    _, _, s_re, s_im = lax.associative_scan(combine, (ar, ai, bu_re, bu_im), axis=1)
    y = (jnp.einsum('gcp,btgp->btgc', c_re.astype(jnp.float32), s_re)
         - jnp.einsum('gcp,btgp->btgc', c_im.astype(jnp.float32), s_im)
         + d.astype(jnp.float32).reshape(SSM_GROUPS, SSM_GROUP) * uf)
    y = jax.nn.gelu(y.reshape(bsz, L, SSM_WIDTH))
    y = y * jax.nn.sigmoid(y @ w_glu.astype(jnp.float32))
    return y.astype(u.dtype), s_re[:, -1], s_im[:, -1]


def _layer(x, p, k_past, v_past, s_re0, s_im0, W):
    bsz, L, _ = x.shape
    h = _rmsnorm(x, W['norm_mix_pre'])
    proj = h @ W['w_in']
    q, k, v, u, g_attn, g_ssm = jnp.split(
        proj, [SB_WIDTH, 2 * SB_WIDTH, 3 * SB_WIDTH, 3 * SB_WIDTH + SSM_WIDTH,
               3 * SB_WIDTH + SSM_WIDTH + D_MODEL], axis=-1)
    q = q.reshape(bsz, L, SB_HEADS, SB_HEAD_DIM)
    k = k.reshape(bsz, L, SB_HEADS, SB_HEAD_DIM)
    v = v.reshape(bsz, L, SB_HEADS, SB_HEAD_DIM)
    if k_past is None:
        k_all, v_all, q_offset = k, v, 0
    else:
        k_all = jnp.concatenate([k_past, k.astype(k_past.dtype)], axis=1)
        v_all = jnp.concatenate([v_past, v.astype(v_past.dtype)], axis=1)
        q_offset = k_past.shape[1]
    o_attn = _stick_breaking(q, k_all, v_all, q_offset).reshape(bsz, L, SB_WIDTH).astype(x.dtype)
    o_ssm, s_re, s_im = _s5(u, s_re0, s_im0, W['ssm_a_re'], W['ssm_a_im'], W['ssm_log_dt'],
                            W['ssm_b_re'], W['ssm_b_im'], W['ssm_c_re'], W['ssm_c_im'],
                            W['ssm_d'], W['w_glu'])
    merged = (jax.nn.sigmoid(g_attn) * (o_attn @ W['w_branch_attn'])
              + jax.nn.sigmoid(g_ssm) * (o_ssm @ W['w_branch_ssm']))
    x = x + _rmsnorm(merged @ W['w_out'], W['norm_mix_post'])
    f = _rmsnorm(x, W['norm_ffn_pre'])
    f = (jax.nn.silu(f @ W['w_ffn_gate']) * (f @ W['w_ffn_up'])) @ W['w_ffn_down']
    x = x + _rmsnorm(f, W['norm_ffn_post'])
    gate = jax.nn.sigmoid(_rmsnorm(x, W['norm_ple_pre']) @ W['w_ple_gate'])
    pe = gate * (p @ W['w_ple_proj'])
    x = x + _rmsnorm(pe, W['norm_ple_post'])
    return x, k, v, s_re, s_im


def setup_inputs(seed: int = 0) -> dict:
    key = jax.random.key(seed)
    ks = iter(jax.random.split(key, 40))
    f32 = jnp.float32

    def nrm(shape, fan_in):
        return jax.random.normal(next(ks), shape, f32) * (fan_in ** -0.5)

    def gain(shape):
        return 1.0 + 0.05 * jax.random.normal(next(ks), shape, f32)

    a_im_base = math.pi * jnp.arange(SSM_STATE, dtype=f32)
    return {
        'x_prompt': jax.random.normal(next(ks), (BATCH, SEQ, D_MODEL), f32),
        'x_sample': jax.random.normal(next(ks), (DEC_BATCH, DEC_SEQ, D_MODEL), f32),
        'cache_k': jax.random.normal(next(ks), (DEPTH, DEC_BATCH, PAST_LEN, SB_HEADS, SB_HEAD_DIM), f32),
        'cache_v': jax.random.normal(next(ks), (DEPTH, DEC_BATCH, PAST_LEN, SB_HEADS, SB_HEAD_DIM), f32),
        'state_ssm_re': 0.3 * jax.random.normal(next(ks), (DEPTH, DEC_BATCH, SSM_GROUPS, SSM_STATE), f32),
        'state_ssm_im': 0.3 * jax.random.normal(next(ks), (DEPTH, DEC_BATCH, SSM_GROUPS, SSM_STATE), f32),
        'p_prompt': jax.random.normal(next(ks), (DEPTH, BATCH, SEQ, PLE_DIM), f32),
        'p_sample': jax.random.normal(next(ks), (DEPTH, DEC_BATCH, DEC_SEQ, PLE_DIM), f32),
        'norm_mix_pre': gain((DEPTH, D_MODEL)),
        'norm_mix_post': gain((DEPTH, D_MODEL)),
        'w_in': nrm((DEPTH, D_MODEL, IN_WIDTH), D_MODEL),
        'ssm_a_re': -0.5 + 0.01 * jax.random.normal(next(ks), (DEPTH, SSM_GROUPS, SSM_STATE), f32),
        'ssm_a_im': a_im_base + 0.01 * jax.random.normal(next(ks), (DEPTH, SSM_GROUPS, SSM_STATE), f32),
        'ssm_log_dt': jax.random.uniform(next(ks), (DEPTH, SSM_GROUPS), f32,
                                         minval=math.log(1e-3), maxval=math.log(1e-1)),
        'ssm_b_re': nrm((DEPTH, SSM_GROUPS, SSM_STATE, SSM_GROUP), SSM_GROUP),
        'ssm_b_im': nrm((DEPTH, SSM_GROUPS, SSM_STATE, SSM_GROUP), SSM_GROUP),
        'ssm_c_re': nrm((DEPTH, SSM_GROUPS, SSM_GROUP, SSM_STATE), SSM_STATE),
        'ssm_c_im': nrm((DEPTH, SSM_GROUPS, SSM_GROUP, SSM_STATE), SSM_STATE),
        'ssm_d': jax.random.normal(next(ks), (DEPTH, SSM_WIDTH), f32),
        'w_glu': nrm((DEPTH, SSM_WIDTH, SSM_WIDTH), SSM_WIDTH),
        'w_branch_attn': nrm((DEPTH, SB_WIDTH, D_MODEL), SB_WIDTH),
        'w_branch_ssm': nrm((DEPTH, SSM_WIDTH, D_MODEL), SSM_WIDTH),
        'w_out': nrm((DEPTH, D_MODEL, D_MODEL), D_MODEL),
        'norm_ffn_pre': gain((DEPTH, D_MODEL)),
        'norm_ffn_post': gain((DEPTH, D_MODEL)),
        'w_ffn_gate': nrm((DEPTH, D_MODEL, D_FF), D_MODEL),
        'w_ffn_up': nrm((DEPTH, D_MODEL, D_FF), D_MODEL),
        'w_ffn_down': nrm((DEPTH, D_FF, D_MODEL), D_FF),
        'norm_ple_pre': gain((DEPTH, D_MODEL)),
        'norm_ple_post': gain((DEPTH, D_MODEL)),
        'w_ple_gate': nrm((DEPTH, D_MODEL, D_MODEL), D_MODEL),
        'w_ple_proj': nrm((DEPTH, PLE_DIM, D_MODEL), PLE_DIM),
    }


def reference(x_prompt, x_sample, cache_k, cache_v, state_ssm_re, state_ssm_im, p_prompt, p_sample,
              norm_mix_pre, norm_mix_post, w_in, ssm_a_re, ssm_a_im, ssm_log_dt, ssm_b_re, ssm_b_im,
              ssm_c_re, ssm_c_im, ssm_d, w_glu, w_branch_attn, w_branch_ssm, w_out,
              norm_ffn_pre, norm_ffn_post, w_ffn_gate, w_ffn_up, w_ffn_down,
              norm_ple_pre, norm_ple_post, w_ple_gate, w_ple_proj):
    yp, ys = x_prompt, x_sample
    kp_l, vp_l, srp_l, sip_l = [], [], [], []
    ks_l, vs_l, srs_l, sis_l = [], [], [], []
    for i in range(DEPTH):
        W = dict(norm_mix_pre=norm_mix_pre[i], norm_mix_post=norm_mix_post[i], w_in=w_in[i],
                 ssm_a_re=ssm_a_re[i], ssm_a_im=ssm_a_im[i], ssm_log_dt=ssm_log_dt[i],
                 ssm_b_re=ssm_b_re[i], ssm_b_im=ssm_b_im[i], ssm_c_re=ssm_c_re[i], ssm_c_im=ssm_c_im[i],
                 ssm_d=ssm_d[i], w_glu=w_glu[i], w_branch_attn=w_branch_attn[i],
                 w_branch_ssm=w_branch_ssm[i], w_out=w_out[i],
                 norm_ffn_pre=norm_ffn_pre[i], norm_ffn_post=norm_ffn_post[i],
                 w_ffn_gate=w_ffn_gate[i], w_ffn_up=w_ffn_up[i], w_ffn_down=w_ffn_down[i],
                 norm_ple_pre=norm_ple_pre[i], norm_ple_post=norm_ple_post[i],
                 w_ple_gate=w_ple_gate[i], w_ple_proj=w_ple_proj[i])
        zero_state = jnp.zeros((x_prompt.shape[0], SSM_GROUPS, SSM_STATE), jnp.float32)
        yp, kp, vp, srp, sip = _layer(yp, p_prompt[i], None, None, zero_state, zero_state, W)
        ys, kn, vn, srs, sis = _layer(ys, p_sample[i], cache_k[i], cache_v[i],
                                      state_ssm_re[i], state_ssm_im[i], W)
        kp_l.append(kp); vp_l.append(vp); srp_l.append(srp); sip_l.append(sip)
        ks_l.append(kn); vs_l.append(vn); srs_l.append(srs); sis_l.append(sis)
    k_prompt = jnp.stack(kp_l)
    v_prompt = jnp.stack(vp_l)
    ssm_re_prompt = jnp.stack(srp_l)
    ssm_im_prompt = jnp.stack(sip_l)
    k_sample = jnp.stack(ks_l)
    v_sample = jnp.stack(vs_l)
    ssm_re_sample = jnp.stack(srs_l)
    ssm_im_sample = jnp.stack(sis_l)
    return (yp, ys, k_prompt, v_prompt, ssm_re_prompt, ssm_im_prompt,
            k_sample, v_sample, ssm_re_sample, ssm_im_sample)
```

```python
import functools
import math

import jax
import jax.numpy as jnp
from jax import lax
from jax.experimental import pallas as pl
from jax.experimental.pallas import tpu as pltpu

F32 = jnp.float32
BF16 = jnp.bfloat16

D_MODEL = 1024
PLE_DIM = 256
SB_WIDTH = 512
SB_HEAD_DIM = 64
SB_HEADS = 8
SSM_WIDTH = 512
SSM_GROUP = 16
SSM_GROUPS = 32
SSM_STATE = 64
SSM_LANES = SSM_GROUPS * SSM_STATE
D_FF = 2816
IN_WIDTH = 3 * SB_WIDTH + SSM_WIDTH + 2 * D_MODEL
RMS_EPS = 1e-6

LANES = 128
HEADS_PER_BLOCK = LANES // SB_HEAD_DIM
SSM_SLABS = SSM_WIDTH // LANES
SLAB_STATE = SSM_LANES // SSM_SLABS
VMEM_LIMIT = 56 * 1024 * 1024


def _const_spec(shape):
    nd = len(shape)
    return pl.BlockSpec(shape, lambda *_: (0,) * nd, pipeline_mode=pl.Buffered(1))


def _rms(x, gain):
    ms = jnp.mean(x * x, axis=-1, keepdims=True)
    return x * lax.rsqrt(ms + RMS_EPS) * gain


def _sigmoid(x):
    return 1.0 / (1.0 + jnp.exp(-x))


def _dot(a, b):
    return jnp.dot(a, b, preferred_element_type=F32)


def _inproj_kernel(x_ref, gain_ref, w_ref, q_ref, k_ref, v_ref, u_ref, ga_ref, gs_ref, *, nb, tm):
    x = x_ref[...].reshape(nb * tm, D_MODEL)
    h = _rms(x, gain_ref[...]).astype(BF16)

    def proj(lo, width):
        return _dot(h, w_ref[:, lo:lo + width])

    q_ref[...] = (proj(0, SB_WIDTH) * (SB_HEAD_DIM ** -0.5)).astype(BF16).reshape(nb, tm, SB_WIDTH)
    k_ref[...] = proj(SB_WIDTH, SB_WIDTH).reshape(nb, tm, SB_WIDTH)
    v_ref[...] = proj(2 * SB_WIDTH, SB_WIDTH).reshape(nb, tm, SB_WIDTH)
    u = proj(3 * SB_WIDTH, SSM_WIDTH)
    for b in range(nb):
        u_ref[:, b * SSM_WIDTH:(b + 1) * SSM_WIDTH] = u[b * tm:(b + 1) * tm, :]
    off = 3 * SB_WIDTH + SSM_WIDTH
    ga_ref[...] = proj(off, D_MODEL).reshape(nb, tm, D_MODEL)
    gs_ref[...] = proj(off + D_MODEL, D_MODEL).reshape(nb, tm, D_MODEL)


def _inproj(x, gain, w_in, *, nb, tm):
    bsz, L, _ = x.shape
    grid = (bsz // nb, L // tm)
    tok = lambda width: pl.BlockSpec((nb, tm, width), lambda b, i: (b, i, 0))
    return pl.pallas_call(
        functools.partial(_inproj_kernel, nb=nb, tm=tm),
        grid=grid,
        in_specs=[tok(D_MODEL), _const_spec((1, D_MODEL)), _const_spec((D_MODEL, IN_WIDTH))],
        out_specs=[tok(SB_WIDTH), tok(SB_WIDTH), tok(SB_WIDTH),
                   pl.BlockSpec((tm, nb * SSM_WIDTH), lambda b, i: (i, b)),
                   tok(D_MODEL), tok(D_MODEL)],
        out_shape=[jax.ShapeDtypeStruct((bsz, L, SB_WIDTH), BF16),
                   jax.ShapeDtypeStruct((bsz, L, SB_WIDTH), F32),
                   jax.ShapeDtypeStruct((bsz, L, SB_WIDTH), F32),
                   jax.ShapeDtypeStruct((L, bsz * SSM_WIDTH), F32),
                   jax.ShapeDtypeStruct((bsz, L, D_MODEL), F32),
                   jax.ShapeDtypeStruct((bsz, L, D_MODEL), F32)],
        compiler_params=pltpu.CompilerParams(
            dimension_semantics=("parallel", "parallel"), vmem_limit_bytes=VMEM_LIMIT),
        name="inproj",
    )(x, gain, w_in)


def _softplus(z):
    return jnp.maximum(z, 0.0) + jnp.log(1.0 + jnp.exp(-jnp.abs(z)))


def _sb_step(qh, kb, vb, tri, c, acc, visible):
    z = lax.dot_general(qh, kb, (((1,), (1,)), ((), ())), preferred_element_type=F32)
    sp = _softplus(z)
    if visible is not None:
        sp = jnp.where(visible, sp, 0.0)
    hi = sp.astype(BF16)
    lo = (sp - hi.astype(F32)).astype(BF16)
    cum = _dot(hi, tri) + _dot(lo, tri)
    w = jnp.exp(z - cum - c)
    if visible is not None:
        w = jnp.where(visible, w, 0.0)
    acc = acc + _dot(w.astype(BF16), vb)
    return c + cum[:, 0:1], acc


def _attn_kernel(*refs, bq, n_past, bk_past):
    if n_past:
        q_ref, k_ref, v_ref, kp_ref, vp_ref, o_ref = refs
    else:
        q_ref, k_ref, v_ref, o_ref = refs
    i = pl.program_id(2)
    q = q_ref[...]
    lane = lax.broadcasted_iota(jnp.int32, (bq, LANES), 1)
    row = lax.broadcasted_iota(jnp.int32, (bq, bq), 0)
    col = lax.broadcasted_iota(jnp.int32, (bq, bq), 1)
    tri = (row >= col).astype(BF16)
    diag_visible = col < row
    if n_past:
        prow = lax.broadcasted_iota(jnp.int32, (bk_past, bk_past), 0)
        pcol = lax.broadcasted_iota(jnp.int32, (bk_past, bk_past), 1)
        tri_past = (prow >= pcol).astype(BF16)

    out = jnp.zeros((bq, LANES), F32)
    for h in range(HEADS_PER_BLOCK):
        in_head = (lane >= h * SB_HEAD_DIM) & (lane < (h + 1) * SB_HEAD_DIM)
        qh = jnp.where(in_head, q, jnp.zeros_like(q))
        c = jnp.zeros((bq, 1), F32)
        acc = jnp.zeros((bq, LANES), F32)

        start = pl.multiple_of(i * bq, bq)
        c, acc = _sb_step(qh, k_ref[pl.ds(start, bq), :].astype(BF16),
                          v_ref[pl.ds(start, bq), :].astype(BF16), tri, c, acc, diag_visible)

        def earlier(jj, carry):
            s0 = pl.multiple_of((i - 1 - jj) * bq, bq)
            return _sb_step(qh, k_ref[pl.ds(s0, bq), :].astype(BF16),
                            v_ref[pl.ds(s0, bq), :].astype(BF16), tri, *carry, None)

        c, acc = lax.fori_loop(0, i, earlier, (c, acc))

        if n_past:
            def past(jj, carry):
                s0 = pl.multiple_of((n_past - 1 - jj) * bk_past, bk_past)
                return _sb_step(qh, kp_ref[pl.ds(s0, bk_past), :].astype(BF16),
                                vp_ref[pl.ds(s0, bk_past), :].astype(BF16), tri_past, *carry, None)

            c, acc = lax.fori_loop(0, n_past, past, (c, acc))

        out = jnp.where(in_head, acc, out)
    o_ref[...] = out.astype(o_ref.dtype)


def _attention(q, k, v, k_past, v_past, *, bq, bk_past=256):
    bsz, L, _ = q.shape
    n_pairs = SB_WIDTH // LANES
    grid = (bsz, n_pairs, L // bq)
    qspec = pl.BlockSpec((None, bq, LANES), lambda b, p, i: (b, i, p))
    kvspec = pl.BlockSpec((None, L, LANES), lambda b, p, i: (b, 0, p))
    in_specs = [qspec, kvspec, kvspec]
    args = [q, k, v]
    n_past = 0
    if k_past is not None:
        P = k_past.shape[1]
        n_past = P // bk_past
        pspec = pl.BlockSpec((None, P, LANES), lambda b, p, i: (b, 0, p))
        in_specs += [pspec, pspec]
        args += [k_past, v_past]
    return pl.pallas_call(
        functools.partial(_attn_kernel, bq=bq, n_past=n_past, bk_past=bk_past),
        grid=grid,
        in_specs=in_specs,
        out_specs=qspec,
        out_shape=jax.ShapeDtypeStruct((bsz, L, SB_WIDTH), BF16),
        compiler_params=pltpu.CompilerParams(
            dimension_semantics=("parallel", "parallel", "parallel"), vmem_limit_bytes=VMEM_LIMIT),
        name="sb_attention",
    )(*args)


def _gelu_tanh(x):
    return 0.5 * x * (1.0 + jnp.tanh(math.sqrt(2.0 / math.pi) * (x + 0.044715 * (x * x * x))))


def _ssm_kernel(u_ref, s0re_ref, s0im_ref, lbre_ref, lbim_ref, bre_ref, bim_ref, cre_ref, cimn_ref,
                d_ref, wglu_ref, o_ref, sfre_ref, sfim_ref, sre, sim, st_re, st_im, *, bsz, tc, lw):
    step = pl.program_id(0)

    @pl.when(step == 0)
    def _():
        st_re[...] = s0re_ref[...]
        st_im[...] = s0im_ref[...]

    u = u_ref[...]
    ub = u.astype(BF16)
    for m in range(SSM_SLABS):
        um = ub[:, m * LANES:(m + 1) * LANES]
        sre[:, m * SLAB_STATE:(m + 1) * SLAB_STATE] = _dot(um, bre_ref[m])
        sim[:, m * SLAB_STATE:(m + 1) * SLAB_STATE] = _dot(um, bim_ref[m])

    for lc in range(SSM_LANES // lw):
        lanes = slice(lc * lw, (lc + 1) * lw)
        lr = jnp.broadcast_to(lbre_ref[:, lanes], (8, lw))
        li = jnp.broadcast_to(lbim_ref[:, lanes], (8, lw))
        for sb in range(bsz // 8):
            subl = slice(sb * 8, (sb + 1) * 8)

            def body(t, carry):
                sr, si = carry
                r0 = pl.multiple_of(t * bsz + sb * 8, 8)
                nr = lr * sr - li * si + sre[pl.ds(r0, 8), lanes]
                ni = lr * si + li * sr + sim[pl.ds(r0, 8), lanes]
                sre[pl.ds(r0, 8), lanes] = nr
                sim[pl.ds(r0, 8), lanes] = ni
                return nr, ni

            sr, si = lax.fori_loop(0, tc, body, (st_re[subl, lanes], st_im[subl, lanes]), unroll=4)
            st_re[subl, lanes] = sr
            st_im[subl, lanes] = si

    ys = []
    for m in range(SSM_SLABS):
        slab = slice(m * SLAB_STATE, (m + 1) * SLAB_STATE)
        ys.append(_dot(sre[:, slab].astype(BF16), cre_ref[m]) + _dot(sim[:, slab].astype(BF16), cimn_ref[m]))
    y = jnp.concatenate(ys, axis=1) + d_ref[...] * u
    y = _gelu_tanh(y)
    o_ref[...] = (y * _sigmoid(_dot(y.astype(BF16), wglu_ref[...]))).astype(o_ref.dtype)

    @pl.when(step == pl.num_programs(0) - 1)
    def _():
        sfre_ref[...] = st_re[...]
        sfim_ref[...] = st_im[...]


def _ssm(u_tm, s0_re, s0_im, prm, *, bsz, tc, lw=512):
    rows = u_tm.shape[0]
    blk = tc * bsz
    row_spec = pl.BlockSpec((blk, SSM_WIDTH), lambda s: (s, 0))
    st_shape = (bsz, SSM_LANES)
    return pl.pallas_call(
        functools.partial(_ssm_kernel, bsz=bsz, tc=tc, lw=lw),
        grid=(rows // blk,),
        in_specs=[row_spec, _const_spec(st_shape), _const_spec(st_shape),
                  _const_spec((1, SSM_LANES)), _const_spec((1, SSM_LANES)),
                  _const_spec((SSM_SLABS, LANES, SLAB_STATE)), _const_spec((SSM_SLABS, LANES, SLAB_STATE)),
                  _const_spec((SSM_SLABS, SLAB_STATE, LANES)), _const_spec((SSM_SLABS, SLAB_STATE, LANES)),
                  _const_spec((1, SSM_WIDTH)), _const_spec((SSM_WIDTH, SSM_WIDTH))],
        out_specs=[row_spec, _const_spec(st_shape), _const_spec(st_shape)],
        out_shape=[jax.ShapeDtypeStruct((rows, SSM_WIDTH), BF16),
                   jax.ShapeDtypeStruct(st_shape, F32), jax.ShapeDtypeStruct(st_shape, F32)],
        scratch_shapes=[pltpu.VMEM((blk, SSM_LANES), F32), pltpu.VMEM((blk, SSM_LANES), F32),
                        pltpu.VMEM(st_shape, F32), pltpu.VMEM(st_shape, F32)],
        compiler_params=pltpu.CompilerParams(
            dimension_semantics=("arbitrary",), vmem_limit_bytes=VMEM_LIMIT),
        name="s5_scan",
    )(u_tm, s0_re, s0_im, prm["lb_re"], prm["lb_im"], prm["bd_re"], prm["bd_im"],
      prm["cd_re"], prm["cd_im_neg"], prm["d"], prm["w_glu"])


def _ssm_params(a_re, a_im, log_dt, b_re, b_im, c_re, c_im, d, w_glu):
    dt = jnp.exp(log_dt)[:, None]
    mag = jnp.exp(a_re * dt)
    lb_re = mag * jnp.cos(a_im * dt)
    lb_im = mag * jnp.sin(a_im * dt)
    den = a_re * a_re + a_im * a_im
    nr, ni = lb_re - 1.0, lb_im
    f_re = (nr * a_re + ni * a_im) / den
    f_im = (ni * a_re - nr * a_im) / den
    bb_re = f_re[:, :, None] * b_re - f_im[:, :, None] * b_im
    bb_im = f_re[:, :, None] * b_im + f_im[:, :, None] * b_re
    gps = SSM_GROUPS // SSM_SLABS
    eye = jnp.eye(gps, dtype=F32)

    def b_slabs(bb):
        t = bb.transpose(0, 2, 1).reshape(SSM_SLABS, gps, SSM_GROUP, SSM_STATE)
        return jnp.einsum("mgcp,gh->mgchp", t, eye).reshape(SSM_SLABS, LANES, SLAB_STATE).astype(BF16)

    def c_slabs(c):
        t = c.transpose(0, 2, 1).reshape(SSM_SLABS, gps, SSM_STATE, SSM_GROUP)
        return jnp.einsum("mgpc,gh->mgphc", t, eye).reshape(SSM_SLABS, SLAB_STATE, LANES).astype(BF16)

    return dict(lb_re=lb_re.reshape(1, SSM_LANES), lb_im=lb_im.reshape(1, SSM_LANES),
                bd_re=b_slabs(bb_re), bd_im=b_slabs(bb_im),
                cd_re=c_slabs(c_re), cd_im_neg=c_slabs(-c_im),
                d=d.reshape(1, SSM_WIDTH), w_glu=w_glu.astype(BF16))


def _post_kernel(x_ref, oa_ref, os_ref, ga_ref, gs_ref, p_ref,
                 wba_ref, wbs_ref, wout_ref, wg_ref, wu_ref, wd_ref, wpg_ref, wpp_ref,
                 n_mix_post, n_ffn_pre, n_ffn_post, n_ple_pre, n_ple_post,
                 y_ref, *, nb, tm, ff_chunk):
    rows = nb * tm
    x = x_ref[...].reshape(rows, D_MODEL)
    oa = oa_ref[...].reshape(rows, SB_WIDTH)
    if nb == 1:
        os_ = os_ref[...]
    else:
        os_ = jnp.concatenate([os_ref[:, b * SSM_WIDTH:(b + 1) * SSM_WIDTH] for b in range(nb)], axis=0)
    merged = (_sigmoid(ga_ref[...].reshape(rows, D_MODEL)) * _dot(oa, wba_ref[...])
              + _sigmoid(gs_ref[...].reshape(rows, D_MODEL)) * _dot(os_, wbs_ref[...]))
    x = x + _rms(_dot(merged.astype(BF16), wout_ref[...]), n_mix_post[...])

    f = _rms(x, n_ffn_pre[...]).astype(BF16)
    ff = jnp.zeros((rows, D_MODEL), F32)
    for c in range(D_FF // ff_chunk):
        cols = slice(c * ff_chunk, (c + 1) * ff_chunk)
        g = _dot(f, wg_ref[:, cols])
        a = (g * _sigmoid(g)) * _dot(f, wu_ref[:, cols])
        ff = ff + _dot(a.astype(BF16), wd_ref[cols, :])
    x = x + _rms(ff, n_ffn_post[...])

    gate = _sigmoid(_dot(_rms(x, n_ple_pre[...]).astype(BF16), wpg_ref[...]))
    pe = gate * _dot(p_ref[...].reshape(rows, PLE_DIM).astype(BF16), wpp_ref[...])
    y_ref[...] = (x + _rms(pe, n_ple_post[...])).reshape(nb, tm, D_MODEL)


def _post(x, o_attn, o_ssm_tm, g_attn, g_ssm, p, W, *, nb, tm, ff_chunk=256):
    bsz, L, _ = x.shape
    grid = (bsz // nb, L // tm)
    tok = lambda width: pl.BlockSpec((nb, tm, width), lambda b, i: (b, i, 0))
    vec = _const_spec((1, D_MODEL))
    return pl.pallas_call(
        functools.partial(_post_kernel, nb=nb, tm=tm, ff_chunk=ff_chunk),
        grid=grid,
        in_specs=[tok(D_MODEL), tok(SB_WIDTH),
                  pl.BlockSpec((tm, nb * SSM_WIDTH), lambda b, i: (i, b)),
                  tok(D_MODEL), tok(D_MODEL), tok(PLE_DIM),
                  _const_spec((SB_WIDTH, D_MODEL)), _const_spec((SSM_WIDTH, D_MODEL)),
                  _const_spec((D_MODEL, D_MODEL)),
                  _const_spec((D_MODEL, D_FF)), _const_spec((D_MODEL, D_FF)), _const_spec((D_FF, D_MODEL)),
                  _const_spec((D_MODEL, D_MODEL)), _const_spec((PLE_DIM, D_MODEL)),
                  vec, vec, vec, vec, vec],
        out_specs=tok(D_MODEL),
        out_shape=jax.ShapeDtypeStruct((bsz, L, D_MODEL), F32),
        compiler_params=pltpu.CompilerParams(
            dimension_semantics=("parallel", "parallel"), vmem_limit_bytes=VMEM_LIMIT),
        name="post",
    )(x, o_attn, o_ssm_tm, g_attn, g_ssm, p,
      W["w_branch_attn"], W["w_branch_ssm"], W["w_out"], W["w_ffn_gate"], W["w_ffn_up"], W["w_ffn_down"],
      W["w_ple_gate"], W["w_ple_proj"],
      W["norm_mix_post"], W["norm_ffn_pre"], W["norm_ffn_post"], W["norm_ple_pre"], W["norm_ple_post"])


def _layer(x, p, k_past, v_past, s_re0, s_im0, W, *, nb, tm, bq, tc, post_nb, post_tm):
    bsz, L, _ = x.shape
    q, k, v, u_tm, g_attn, g_ssm = _inproj(x, W["norm_mix_pre"], W["w_in"], nb=nb, tm=tm)
    o_attn = _attention(q, k, v, k_past, v_past, bq=bq)
    o_ssm_tm, s_re, s_im = _ssm(u_tm.reshape(L * bsz, SSM_WIDTH), s_re0.reshape(bsz, SSM_LANES),
                                s_im0.reshape(bsz, SSM_LANES), W["ssm"], bsz=bsz, tc=tc)
    y = _post(x, o_attn, o_ssm_tm.reshape(L, bsz * SSM_WIDTH), g_attn, g_ssm, p, W, nb=post_nb, tm=post_tm)
    heads = (bsz, L, SB_HEADS, SB_HEAD_DIM)
    state = (bsz, SSM_GROUPS, SSM_STATE)
    return y, k.reshape(heads), v.reshape(heads), s_re.reshape(state), s_im.reshape(state)


def kernel(x_prompt, x_sample, cache_k, cache_v, state_ssm_re, state_ssm_im, p_prompt, p_sample, norm_mix_pre, norm_mix_post, w_in, ssm_a_re, ssm_a_im, ssm_log_dt, ssm_b_re, ssm_b_im, ssm_c_re, ssm_c_im, ssm_d, w_glu, w_branch_attn, w_branch_ssm, w_out, norm_ffn_pre, norm_ffn_post, w_ffn_gate, w_ffn_up, w_ffn_down, norm_ple_pre, norm_ple_post, w_ple_gate, w_ple_proj):
    depth = w_in.shape[0]
    yp, ys = x_prompt, x_sample
    outs = [[] for _ in range(8)]
    for i in range(depth):
        W = dict(
            norm_mix_pre=norm_mix_pre[i][None], norm_mix_post=norm_mix_post[i][None],
            norm_ffn_pre=norm_ffn_pre[i][None], norm_ffn_post=norm_ffn_post[i][None],
            norm_ple_pre=norm_ple_pre[i][None], norm_ple_post=norm_ple_post[i][None],
            w_in=w_in[i].astype(BF16), w_branch_attn=w_branch_attn[i].astype(BF16),
            w_branch_ssm=w_branch_ssm[i].astype(BF16), w_out=w_out[i].astype(BF16),
            w_ffn_gate=w_ffn_gate[i].astype(BF16), w_ffn_up=w_ffn_up[i].astype(BF16),
            w_ffn_down=w_ffn_down[i].astype(BF16), w_ple_gate=w_ple_gate[i].astype(BF16),
            w_ple_proj=w_ple_proj[i].astype(BF16),
            ssm=_ssm_params(ssm_a_re[i], ssm_a_im[i], ssm_log_dt[i], ssm_b_re[i], ssm_b_im[i],
                            ssm_c_re[i], ssm_c_im[i], ssm_d[i], w_glu[i]))
        bp, lp = yp.shape[0], yp.shape[1]
        bs, ls = ys.shape[0], ys.shape[1]
        zero_state = jnp.zeros((bp, SSM_GROUPS, SSM_STATE), F32)
        yp, kp, vp, srp, sip = _layer(
            yp, p_prompt[i], None, None, zero_state, zero_state, W,
            nb=1, tm=min(512, lp), bq=min(256, lp), tc=512 // bp, post_nb=1, post_tm=min(256, lp))
        past = cache_k.shape[2]
        kc = cache_k[i].reshape(bs, past, SB_WIDTH)
        vc = cache_v[i].reshape(bs, past, SB_WIDTH)
        ys, kn, vn, srs, sis = _layer(
            ys, p_sample[i], kc, vc, state_ssm_re[i], state_ssm_im[i], W,
            nb=512 // ls, tm=ls, bq=ls, tc=512 // bs, post_nb=256 // ls, post_tm=ls)
        for lst, val in zip(outs, (kp, vp, srp, sip, kn, vn, srs, sis)):
            lst.append(val)
    return (yp, ys) + tuple(jnp.stack(o) for o in outs)
```

```python
import functools
import math

import jax
import jax.numpy as jnp
from jax import lax
from jax.experimental import pallas as pl
from jax.experimental.pallas import tpu as pltpu

F32 = jnp.float32
BF16 = jnp.bfloat16

D_MODEL = 1024
PLE_DIM = 256
SB_WIDTH = 512
SB_HEAD_DIM = 64
SB_HEADS = 8
SSM_WIDTH = 512
SSM_GROUP = 16
SSM_GROUPS = 32
SSM_STATE = 64
SSM_LANES = SSM_GROUPS * SSM_STATE
D_FF = 2816
IN_WIDTH = 3 * SB_WIDTH + SSM_WIDTH + 2 * D_MODEL
RMS_EPS = 1e-6

LANES = 128
HEADS_PER_BLOCK = LANES // SB_HEAD_DIM
SSM_SLABS = SSM_WIDTH // LANES
SLAB_STATE = SSM_LANES // SSM_SLABS
VMEM_LIMIT = 56 * 1024 * 1024


def _const_spec(shape):
    nd = len(shape)
    return pl.BlockSpec(shape, lambda *_: (0,) * nd, pipeline_mode=pl.Buffered(1))


def _rms(x, gain):
    ms = jnp.mean(x * x, axis=-1, keepdims=True)
    return x * lax.rsqrt(ms + RMS_EPS) * gain


def _sigmoid(x):
    return 1.0 / (1.0 + jnp.exp(-x))


def _dot(a, b):
    return jnp.dot(a, b, preferred_element_type=F32)


def _inproj_kernel(x_ref, gain_ref, w_ref, q_ref, k_ref, v_ref, u_ref, ga_ref, gs_ref, *, nb, tm):
    x = x_ref[...].reshape(nb * tm, D_MODEL)
    h = _rms(x, gain_ref[...]).astype(BF16)

    def proj(lo, width):
        return _dot(h, w_ref[:, lo:lo + width])

    q_ref[...] = (proj(0, SB_WIDTH) * (SB_HEAD_DIM ** -0.5)).astype(BF16).reshape(nb, tm, SB_WIDTH)
    k_ref[...] = proj(SB_WIDTH, SB_WIDTH).reshape(nb, tm, SB_WIDTH)
    v_ref[...] = proj(2 * SB_WIDTH, SB_WIDTH).reshape(nb, tm, SB_WIDTH)
    u = proj(3 * SB_WIDTH, SSM_WIDTH)
    for b in range(nb):
        u_ref[:, b * SSM_WIDTH:(b + 1) * SSM_WIDTH] = u[b * tm:(b + 1) * tm, :]
    off = 3 * SB_WIDTH + SSM_WIDTH
    ga_ref[...] = proj(off, D_MODEL).reshape(nb, tm, D_MODEL)
    gs_ref[...] = proj(off + D_MODEL, D_MODEL).reshape(nb, tm, D_MODEL)


def _inproj(x, gain, w_in, *, nb, tm):
    bsz, L, _ = x.shape
    grid = (bsz // nb, L // tm)
    tok = lambda width: pl.BlockSpec((nb, tm, width), lambda b, i: (b, i, 0))
    return pl.pallas_call(
        functools.partial(_inproj_kernel, nb=nb, tm=tm),
        grid=grid,
        in_specs=[tok(D_MODEL), _const_spec((1, D_MODEL)), _const_spec((D_MODEL, IN_WIDTH))],
        out_specs=[tok(SB_WIDTH), tok(SB_WIDTH), tok(SB_WIDTH),
                   pl.BlockSpec((tm, nb * SSM_WIDTH), lambda b, i: (i, b)),
                   tok(D_MODEL), tok(D_MODEL)],
        out_shape=[jax.ShapeDtypeStruct((bsz, L, SB_WIDTH), BF16),
                   jax.ShapeDtypeStruct((bsz, L, SB_WIDTH), F32),
                   jax.ShapeDtypeStruct((bsz, L, SB_WIDTH), F32),
                   jax.ShapeDtypeStruct((L, bsz * SSM_WIDTH), F32),
                   jax.ShapeDtypeStruct((bsz, L, D_MODEL), F32),
                   jax.ShapeDtypeStruct((bsz, L, D_MODEL), F32)],
        compiler_params=pltpu.CompilerParams(
            dimension_semantics=("parallel", "parallel"), vmem_limit_bytes=VMEM_LIMIT),
        name="inproj",
    )(x, gain, w_in)


def _softplus(z):
    return jnp.maximum(z, 0.0) + jnp.log(1.0 + jnp.exp(-jnp.abs(z)))


SKIP_LOG = 105.0


def _sb_scores(qh, kb, tri, visible):
    z = lax.dot_general(qh, kb, (((1,), (1,)), ((), ())), preferred_element_type=F32)
    sp = _softplus(z)
    if visible is not None:
        sp = jnp.where(visible, sp, 0.0)
    hi = sp.astype(BF16)
    lo = (sp - hi.astype(F32)).astype(BF16)
    return z, _dot(hi, tri) + _dot(lo, tri)


def _sb_weights(z, cum, c, visible):
    w = jnp.exp(z - cum - c)
    if visible is not None:
        w = jnp.where(visible, w, 0.0)
    return w.astype(BF16)


def _attn_kernel(*refs, bq, n_past, bk_past):
    if n_past:
        q_ref, k_ref, v_ref, kp_ref, vp_ref, o_ref = refs
    else:
        q_ref, k_ref, v_ref, o_ref = refs
    i = pl.program_id(2)
    q = q_ref[...]
    lane = lax.broadcasted_iota(jnp.int32, (bq, LANES), 1)
    row = lax.broadcasted_iota(jnp.int32, (bq, bq), 0)
    col = lax.broadcasted_iota(jnp.int32, (bq, bq), 1)
    tri = (row >= col).astype(BF16)
    diag_visible = col < row
    if n_past:
        prow = lax.broadcasted_iota(jnp.int32, (bk_past, bk_past), 0)
        pcol = lax.broadcasted_iota(jnp.int32, (bk_past, bk_past), 1)
        tri_e, bk, ke_ref, ve_ref = (prow >= pcol).astype(BF16), bk_past, kp_ref, vp_ref
        n_earlier = n_past
    else:
        tri_e, bk, ke_ref, ve_ref = tri, bq, k_ref, v_ref
        n_earlier = i

    in_head = [(lane >= h * SB_HEAD_DIM) & (lane < (h + 1) * SB_HEAD_DIM) for h in range(HEADS_PER_BLOCK)]
    qs = [jnp.where(m, q, jnp.zeros_like(q)) for m in in_head]

    def earlier_block(j):
        s0 = pl.multiple_of(j * bk, bk)
        return ke_ref[pl.ds(s0, bk), :].astype(BF16), ve_ref[pl.ds(s0, bk), :].astype(BF16)

    def first_blocks(with_earlier):
        start = pl.multiple_of(i * bq, bq)
        kd = k_ref[pl.ds(start, bq), :].astype(BF16)
        vd = v_ref[pl.ds(start, bq), :].astype(BF16)
        if with_earlier:
            ke, ve = earlier_block(n_earlier - 1)
        res = []
        for qh in qs:
            zd, cumd = _sb_scores(qh, kd, tri, diag_visible)
            c = cumd[:, 0:1]
            acc = _dot(_sb_weights(zd, cumd, 0.0, diag_visible), vd)
            if with_earlier:
                ze, cume = _sb_scores(qh, ke, tri_e, None)
                acc = acc + _dot(_sb_weights(ze, cume, c, None), ve)
                c = c + cume[:, 0:1]
            res += [c, acc]
        return tuple(res)

    def unfinished(c0, c1):
        return (jnp.minimum(jnp.min(c0), jnp.min(c1)) < SKIP_LOG).astype(jnp.int32)

    if n_past:
        c0, a0, c1, a1 = first_blocks(True)
    else:
        c0, a0, c1, a1 = lax.cond(i > 0, lambda: first_blocks(True), lambda: first_blocks(False))

    def more(st):
        j, go = st[0], st[1]
        return (j >= 0) & (go > 0)

    def visit(st):
        j, _, c0, a0, c1, a1 = st
        kb, vb = earlier_block(j)
        out = []
        for qh, c, acc in ((qs[0], c0, a0), (qs[1], c1, a1)):
            z, cum = _sb_scores(qh, kb, tri_e, None)
            out += [c + cum[:, 0:1], acc + _dot(_sb_weights(z, cum, c, None), vb)]
        return (j - 1, unfinished(out[0], out[2])) + tuple(out)

    st = lax.while_loop(more, visit, (n_earlier - 2, unfinished(c0, c1), c0, a0, c1, a1))
    o_ref[...] = jnp.where(in_head[0], st[3], st[5]).astype(o_ref.dtype)


def _attention(q, k, v, k_past, v_past, *, bq, bk_past=256):
    bsz, L, _ = q.shape
    n_pairs = SB_WIDTH // LANES
    grid = (bsz, n_pairs, L // bq)
    qspec = pl.BlockSpec((None, bq, LANES), lambda b, p, i: (b, i, p))
    kvspec = pl.BlockSpec((None, L, LANES), lambda b, p, i: (b, 0, p))
    in_specs = [qspec, kvspec, kvspec]
    args = [q, k, v]
    n_past = 0
    if k_past is not None:
        P = k_past.shape[1]
        n_past = P // bk_past
        pspec = pl.BlockSpec((None, P, LANES), lambda b, p, i: (b, 0, p))
        in_specs += [pspec, pspec]
        args += [k_past, v_past]
    return pl.pallas_call(
        functools.partial(_attn_kernel, bq=bq, n_past=n_past, bk_past=bk_past),
        grid=grid,
        in_specs=in_specs,
        out_specs=qspec,
        out_shape=jax.ShapeDtypeStruct((bsz, L, SB_WIDTH), BF16),
        compiler_params=pltpu.CompilerParams(
            dimension_semantics=("parallel", "parallel", "parallel"), vmem_limit_bytes=VMEM_LIMIT),
        name="sb_attention",
    )(*args)


def _gelu_tanh(x):
    return 0.5 * x * (1.0 + jnp.tanh(math.sqrt(2.0 / math.pi) * (x + 0.044715 * (x * x * x))))


def _ssm_kernel(u_ref, s0re_ref, s0im_ref, lbre_ref, lbim_ref, bre_ref, bim_ref, cre_ref, cimn_ref,
                d_ref, wglu_ref, o_ref, sfre_ref, sfim_ref, sre, sim, st_re, st_im, *, bsz, tc, lw):
    step = pl.program_id(0)

    @pl.when(step == 0)
    def _():
        st_re[...] = s0re_ref[...]
        st_im[...] = s0im_ref[...]

    u = u_ref[...]
    ub = u.astype(BF16)
    for m in range(SSM_SLABS):
        um = ub[:, m * LANES:(m + 1) * LANES]
        sre[:, m * SLAB_STATE:(m + 1) * SLAB_STATE] = _dot(um, bre_ref[m])
        sim[:, m * SLAB_STATE:(m + 1) * SLAB_STATE] = _dot(um, bim_ref[m])

    for lc in range(SSM_LANES // lw):
        lanes = slice(lc * lw, (lc + 1) * lw)
        lr = jnp.broadcast_to(lbre_ref[:, lanes], (8, lw))
        li = jnp.broadcast_to(lbim_ref[:, lanes], (8, lw))
        for sb in range(bsz // 8):
            subl = slice(sb * 8, (sb + 1) * 8)

            def body(t, carry):
                sr, si = carry
                r0 = pl.multiple_of(t * bsz + sb * 8, 8)
                nr = lr * sr - li * si + sre[pl.ds(r0, 8), lanes]
                ni = lr * si + li * sr + sim[pl.ds(r0, 8), lanes]
                sre[pl.ds(r0, 8), lanes] = nr
                sim[pl.ds(r0, 8), lanes] = ni
                return nr, ni

            sr, si = lax.fori_loop(0, tc, body, (st_re[subl, lanes], st_im[subl, lanes]), unroll=4)
            st_re[subl, lanes] = sr
            st_im[subl, lanes] = si

    ys = []
    for m in range(SSM_SLABS):
        slab = slice(m * SLAB_STATE, (m + 1) * SLAB_STATE)
        ys.append(_dot(sre[:, slab].astype(BF16), cre_ref[m]) + _dot(sim[:, slab].astype(BF16), cimn_ref[m]))
    y = jnp.concatenate(ys, axis=1) + d_ref[...] * u
    y = _gelu_tanh(y)
    o_ref[...] = (y * _sigmoid(_dot(y.astype(BF16), wglu_ref[...]))).astype(o_ref.dtype)

    @pl.when(step == pl.num_programs(0) - 1)
    def _():
        sfre_ref[...] = st_re[...]
        sfim_ref[...] = st_im[...]


def _ssm(u_tm, s0_re, s0_im, prm, *, bsz, tc, lw=512):
    rows = u_tm.shape[0]
    blk = tc * bsz
    row_spec = pl.BlockSpec((blk, SSM_WIDTH), lambda s: (s, 0))
    st_shape = (bsz, SSM_LANES)
    return pl.pallas_call(
        functools.partial(_ssm_kernel, bsz=bsz, tc=tc, lw=lw),
        grid=(rows // blk,),
        in_specs=[row_spec, _const_spec(st_shape), _const_spec(st_shape),
                  _const_spec((1, SSM_LANES)), _const_spec((1, SSM_LANES)),
                  _const_spec((SSM_SLABS, LANES, SLAB_STATE)), _const_spec((SSM_SLABS, LANES, SLAB_STATE)),
                  _const_spec((SSM_SLABS, SLAB_STATE, LANES)), _const_spec((SSM_SLABS, SLAB_STATE, LANES)),
                  _const_spec((1, SSM_WIDTH)), _const_spec((SSM_WIDTH, SSM_WIDTH))],
        out_specs=[row_spec, _const_spec(st_shape), _const_spec(st_shape)],
        out_shape=[jax.ShapeDtypeStruct((rows, SSM_WIDTH), BF16),
                   jax.ShapeDtypeStruct(st_shape, F32), jax.ShapeDtypeStruct(st_shape, F32)],
        scratch_shapes=[pltpu.VMEM((blk, SSM_LANES), F32), pltpu.VMEM((blk, SSM_LANES), F32),
                        pltpu.VMEM(st_shape, F32), pltpu.VMEM(st_shape, F32)],
        compiler_params=pltpu.CompilerParams(
            dimension_semantics=("arbitrary",), vmem_limit_bytes=VMEM_LIMIT),
        name="s5_scan",
    )(u_tm, s0_re, s0_im, prm["lb_re"], prm["lb_im"], prm["bd_re"], prm["bd_im"],
      prm["cd_re"], prm["cd_im_neg"], prm["d"], prm["w_glu"])


def _ssm_params(a_re, a_im, log_dt, b_re, b_im, c_re, c_im, d, w_glu):
    dt = jnp.exp(log_dt)[:, None]
    mag = jnp.exp(a_re * dt)
    lb_re = mag * jnp.cos(a_im * dt)
    lb_im = mag * jnp.sin(a_im * dt)
    den = a_re * a_re + a_im * a_im
    nr, ni = lb_re - 1.0, lb_im
    f_re = (nr * a_re + ni * a_im) / den
    f_im = (ni * a_re - nr * a_im) / den
    bb_re = f_re[:, :, None] * b_re - f_im[:, :, None] * b_im
    bb_im = f_re[:, :, None] * b_im + f_im[:, :, None] * b_re
    gps = SSM_GROUPS // SSM_SLABS
    eye = jnp.eye(gps, dtype=F32)

    def b_slabs(bb):
        t = bb.transpose(0, 2, 1).reshape(SSM_SLABS, gps, SSM_GROUP, SSM_STATE)
        return jnp.einsum("mgcp,gh->mgchp", t, eye).reshape(SSM_SLABS, LANES, SLAB_STATE).astype(BF16)

    def c_slabs(c):
        t = c.transpose(0, 2, 1).reshape(SSM_SLABS, gps, SSM_STATE, SSM_GROUP)
        return jnp.einsum("mgpc,gh->mgphc", t, eye).reshape(SSM_SLABS, SLAB_STATE, LANES).astype(BF16)

    return dict(lb_re=lb_re.reshape(1, SSM_LANES), lb_im=lb_im.reshape(1, SSM_LANES),
                bd_re=b_slabs(bb_re), bd_im=b_slabs(bb_im),
                cd_re=c_slabs(c_re), cd_im_neg=c_slabs(-c_im),
                d=d.reshape(1, SSM_WIDTH), w_glu=w_glu.astype(BF16))


def _post_kernel(x_ref, oa_ref, os_ref, ga_ref, gs_ref, p_ref,
                 wba_ref, wbs_ref, wout_ref, wg_ref, wu_ref, wd_ref, wpg_ref, wpp_ref,
                 n_mix_post, n_ffn_pre, n_ffn_post, n_ple_pre, n_ple_post,
                 y_ref, *, nb, tm, ff_chunk):
    rows = nb * tm
    x = x_ref[...].reshape(rows, D_MODEL)
    oa = oa_ref[...].reshape(rows, SB_WIDTH)
    if nb == 1:
        os_ = os_ref[...]
    else:
        os_ = jnp.concatenate([os_ref[:, b * SSM_WIDTH:(b + 1) * SSM_WIDTH] for b in range(nb)], axis=0)
    merged = (_sigmoid(ga_ref[...].reshape(rows, D_MODEL)) * _dot(oa, wba_ref[...])
              + _sigmoid(gs_ref[...].reshape(rows, D_MODEL)) * _dot(os_, wbs_ref[...]))
    x = x + _rms(_dot(merged.astype(BF16), wout_ref[...]), n_mix_post[...])

    f = _rms(x, n_ffn_pre[...]).astype(BF16)
    ff = jnp.zeros((rows, D_MODEL), F32)
    for c in range(D_FF // ff_chunk):
        cols = slice(c * ff_chunk, (c + 1) * ff_chunk)
        g = _dot(f, wg_ref[:, cols])
        a = (g * _sigmoid(g)) * _dot(f, wu_ref[:, cols])
        ff = ff + _dot(a.astype(BF16), wd_ref[cols, :])
    x = x + _rms(ff, n_ffn_post[...])

    gate = _sigmoid(_dot(_rms(x, n_ple_pre[...]).astype(BF16), wpg_ref[...]))
    pe = gate * _dot(p_ref[...].reshape(rows, PLE_DIM).astype(BF16), wpp_ref[...])
    y_ref[...] = (x + _rms(pe, n_ple_post[...])).reshape(nb, tm, D_MODEL)


def _post(x, o_attn, o_ssm_tm, g_attn, g_ssm, p, W, *, nb, tm, ff_chunk=256):
    bsz, L, _ = x.shape
    grid = (bsz // nb, L // tm)
    tok = lambda width: pl.BlockSpec((nb, tm, width), lambda b, i: (b, i, 0))
    vec = _const_spec((1, D_MODEL))
    return pl.pallas_call(
        functools.partial(_post_kernel, nb=nb, tm=tm, ff_chunk=ff_chunk),
        grid=grid,
        in_specs=[tok(D_MODEL), tok(SB_WIDTH),
                  pl.BlockSpec((tm, nb * SSM_WIDTH), lambda b, i: (i, b)),
                  tok(D_MODEL), tok(D_MODEL), tok(PLE_DIM),
                  _const_spec((SB_WIDTH, D_MODEL)), _const_spec((SSM_WIDTH, D_MODEL)),
                  _const_spec((D_MODEL, D_MODEL)),
                  _const_spec((D_MODEL, D_FF)), _const_spec((D_MODEL, D_FF)), _const_spec((D_FF, D_MODEL)),
                  _const_spec((D_MODEL, D_MODEL)), _const_spec((PLE_DIM, D_MODEL)),
                  vec, vec, vec, vec, vec],
        out_specs=tok(D_MODEL),
        out_shape=jax.ShapeDtypeStruct((bsz, L, D_MODEL), F32),
        compiler_params=pltpu.CompilerParams(
            dimension_semantics=("parallel", "parallel"), vmem_limit_bytes=VMEM_LIMIT),
        name="post",
    )(x, o_attn, o_ssm_tm, g_attn, g_ssm, p,
      W["w_branch_attn"], W["w_branch_ssm"], W["w_out"], W["w_ffn_gate"], W["w_ffn_up"], W["w_ffn_down"],
      W["w_ple_gate"], W["w_ple_proj"],
      W["norm_mix_post"], W["norm_ffn_pre"], W["norm_ffn_post"], W["norm_ple_pre"], W["norm_ple_post"])


def _layer(x, p, k_past, v_past, s_re0, s_im0, W, *, nb, tm, bq, tc, post_nb, post_tm):
    bsz, L, _ = x.shape
    q, k, v, u_tm, g_attn, g_ssm = _inproj(x, W["norm_mix_pre"], W["w_in"], nb=nb, tm=tm)
    o_attn = _attention(q, k, v, k_past, v_past, bq=bq)
    o_ssm_tm, s_re, s_im = _ssm(u_tm.reshape(L * bsz, SSM_WIDTH), s_re0.reshape(bsz, SSM_LANES),
                                s_im0.reshape(bsz, SSM_LANES), W["ssm"], bsz=bsz, tc=tc)
    y = _post(x, o_attn, o_ssm_tm.reshape(L, bsz * SSM_WIDTH), g_attn, g_ssm, p, W, nb=post_nb, tm=post_tm)
    heads = (bsz, L, SB_HEADS, SB_HEAD_DIM)
    state = (bsz, SSM_GROUPS, SSM_STATE)
    return y, k.reshape(heads), v.reshape(heads), s_re.reshape(state), s_im.reshape(state)


def kernel(x_prompt, x_sample, cache_k, cache_v, state_ssm_re, state_ssm_im, p_prompt, p_sample, norm_mix_pre, norm_mix_post, w_in, ssm_a_re, ssm_a_im, ssm_log_dt, ssm_b_re, ssm_b_im, ssm_c_re, ssm_c_im, ssm_d, w_glu, w_branch_attn, w_branch_ssm, w_out, norm_ffn_pre, norm_ffn_post, w_ffn_gate, w_ffn_up, w_ffn_down, norm_ple_pre, norm_ple_post, w_ple_gate, w_ple_proj):
    depth = w_in.shape[0]
    yp, ys = x_prompt, x_sample
    outs = [[] for _ in range(8)]
    for i in range(depth):
        W = dict(
            norm_mix_pre=norm_mix_pre[i][None], norm_mix_post=norm_mix_post[i][None],
            norm_ffn_pre=norm_ffn_pre[i][None], norm_ffn_post=norm_ffn_post[i][None],
            norm_ple_pre=norm_ple_pre[i][None], norm_ple_post=norm_ple_post[i][None],
            w_in=w_in[i].astype(BF16), w_branch_attn=w_branch_attn[i].astype(BF16),
            w_branch_ssm=w_branch_ssm[i].astype(BF16), w_out=w_out[i].astype(BF16),
            w_ffn_gate=w_ffn_gate[i].astype(BF16), w_ffn_up=w_ffn_up[i].astype(BF16),
            w_ffn_down=w_ffn_down[i].astype(BF16), w_ple_gate=w_ple_gate[i].astype(BF16),
            w_ple_proj=w_ple_proj[i].astype(BF16),
            ssm=_ssm_params(ssm_a_re[i], ssm_a_im[i], ssm_log_dt[i], ssm_b_re[i], ssm_b_im[i],
                            ssm_c_re[i], ssm_c_im[i], ssm_d[i], w_glu[i]))
        bp, lp = yp.shape[0], yp.shape[1]
        bs, ls = ys.shape[0], ys.shape[1]
        zero_state = jnp.zeros((bp, SSM_GROUPS, SSM_STATE), F32)
        yp, kp, vp, srp, sip = _layer(
            yp, p_prompt[i], None, None, zero_state, zero_state, W,
            nb=1, tm=min(512, lp), bq=min(256, lp), tc=512 // bp, post_nb=1, post_tm=min(256, lp))
        past = cache_k.shape[2]
        kc = cache_k[i].reshape(bs, past, SB_WIDTH)
        vc = cache_v[i].reshape(bs, past, SB_WIDTH)
        ys, kn, vn, srs, sis = _layer(
            ys, p_sample[i], kc, vc, state_ssm_re[i], state_ssm_im[i], W,
            nb=512 // ls, tm=ls, bq=ls, tc=512 // bs, post_nb=256 // ls, post_tm=ls)
        for lst, val in zip(outs, (kp, vp, srp, sip, kn, vn, srs, sis)):
            lst.append(val)
    return (yp, ys) + tuple(jnp.stack(o) for o in outs)
```

```python
import functools
import math

import jax
import jax.numpy as jnp
from jax import lax
from jax.experimental import pallas as pl
from jax.experimental.pallas import tpu as pltpu

F32 = jnp.float32
BF16 = jnp.bfloat16

D_MODEL = 1024
PLE_DIM = 256
SB_WIDTH = 512
SB_HEAD_DIM = 64
SB_HEADS = 8
SSM_WIDTH = 512
SSM_GROUP = 16
SSM_GROUPS = 32
SSM_STATE = 64
SSM_LANES = SSM_GROUPS * SSM_STATE
D_FF = 2816
IN_WIDTH = 3 * SB_WIDTH + SSM_WIDTH + 2 * D_MODEL
RMS_EPS = 1e-6

LANES = 128
HEADS_PER_BLOCK = LANES // SB_HEAD_DIM
SSM_SLABS = SSM_WIDTH // LANES
SLAB_STATE = SSM_LANES // SSM_SLABS
VMEM_LIMIT = 56 * 1024 * 1024


def _const_spec(shape):
    nd = len(shape)
    return pl.BlockSpec(shape, lambda *_: (0,) * nd, pipeline_mode=pl.Buffered(1))


def _rms(x, gain):
    ms = jnp.mean(x * x, axis=-1, keepdims=True)
    return x * lax.rsqrt(ms + RMS_EPS) * gain


def _sigmoid(x):
    return 1.0 / (1.0 + jnp.exp(-x))


def _dot(a, b):
    return jnp.dot(a, b, preferred_element_type=F32)


def _inproj_kernel(x_ref, gain_ref, w_ref, q_ref, k_ref, v_ref, u_ref, ga_ref, gs_ref, *, nb, tm):
    x = x_ref[...].reshape(nb * tm, D_MODEL)
    h = _rms(x, gain_ref[...]).astype(BF16)

    def proj(lo, width):
        return _dot(h, w_ref[:, lo:lo + width])

    q_ref[...] = (proj(0, SB_WIDTH) * (SB_HEAD_DIM ** -0.5)).astype(BF16).reshape(nb, tm, SB_WIDTH)
    k_ref[...] = proj(SB_WIDTH, SB_WIDTH).reshape(nb, tm, SB_WIDTH)
    v_ref[...] = proj(2 * SB_WIDTH, SB_WIDTH).reshape(nb, tm, SB_WIDTH)
    u = proj(3 * SB_WIDTH, SSM_WIDTH)
    for b in range(nb):
        u_ref[:, b * SSM_WIDTH:(b + 1) * SSM_WIDTH] = u[b * tm:(b + 1) * tm, :]
    off = 3 * SB_WIDTH + SSM_WIDTH
    ga_ref[...] = proj(off, D_MODEL).reshape(nb, tm, D_MODEL)
    gs_ref[...] = proj(off + D_MODEL, D_MODEL).reshape(nb, tm, D_MODEL)


def _inproj(x, gain, w_in, *, nb, tm):
    bsz, L, _ = x.shape
    grid = (bsz // nb, L // tm)
    tok = lambda width: pl.BlockSpec((nb, tm, width), lambda b, i: (b, i, 0))
    return pl.pallas_call(
        functools.partial(_inproj_kernel, nb=nb, tm=tm),
        grid=grid,
        in_specs=[tok(D_MODEL), _const_spec((1, D_MODEL)), _const_spec((D_MODEL, IN_WIDTH))],
        out_specs=[tok(SB_WIDTH), tok(SB_WIDTH), tok(SB_WIDTH),
                   pl.BlockSpec((tm, nb * SSM_WIDTH), lambda b, i: (i, b)),
                   tok(D_MODEL), tok(D_MODEL)],
        out_shape=[jax.ShapeDtypeStruct((bsz, L, SB_WIDTH), BF16),
                   jax.ShapeDtypeStruct((bsz, L, SB_WIDTH), F32),
                   jax.ShapeDtypeStruct((bsz, L, SB_WIDTH), F32),
                   jax.ShapeDtypeStruct((L, bsz * SSM_WIDTH), F32),
                   jax.ShapeDtypeStruct((bsz, L, D_MODEL), F32),
                   jax.ShapeDtypeStruct((bsz, L, D_MODEL), F32)],
        compiler_params=pltpu.CompilerParams(
            dimension_semantics=("parallel", "parallel"), vmem_limit_bytes=VMEM_LIMIT),
        name="inproj",
    )(x, gain, w_in)


def _softplus(z):
    return jnp.maximum(z, 0.0) + jnp.log(1.0 + jnp.exp(-jnp.abs(z)))


SKIP_LOG = 105.0


def _sb_scores(qh, kb, tri, visible):
    z = lax.dot_general(qh, kb, (((1,), (1,)), ((), ())), preferred_element_type=F32)
    sp = _softplus(z)
    if visible is not None:
        sp = jnp.where(visible, sp, 0.0)
    hi = sp.astype(BF16)
    lo = (sp - hi.astype(F32)).astype(BF16)
    return z, _dot(hi, tri) + _dot(lo, tri)


def _sb_weights(z, cum, c, visible):
    w = jnp.exp(z - cum - c)
    if visible is not None:
        w = jnp.where(visible, w, 0.0)
    return w.astype(BF16)


def _unfinished(*cs):
    m = jnp.min(cs[0])
    for c in cs[1:]:
        m = jnp.minimum(m, jnp.min(c))
    return (m < SKIP_LOG).astype(jnp.int32)


def _more(st):
    return (st[0] >= 0) & (st[1] > 0)


def _attn_kernel(q_ref, k_ref, v_ref, o_ref, *, bq):
    i = pl.program_id(2)
    q = q_ref[...]
    lane = lax.broadcasted_iota(jnp.int32, (bq, LANES), 1)
    row = lax.broadcasted_iota(jnp.int32, (bq, bq), 0)
    col = lax.broadcasted_iota(jnp.int32, (bq, bq), 1)
    tri = (row >= col).astype(BF16)
    diag_visible = col < row

    in_head = [(lane >= h * SB_HEAD_DIM) & (lane < (h + 1) * SB_HEAD_DIM) for h in range(HEADS_PER_BLOCK)]
    qs = [jnp.where(m, q, jnp.zeros_like(q)) for m in in_head]

    def block(j):
        s0 = pl.multiple_of(j * bq, bq)
        return k_ref[pl.ds(s0, bq), :].astype(BF16), v_ref[pl.ds(s0, bq), :].astype(BF16)

    def first_blocks(with_earlier):
        kd, vd = block(i)
        if with_earlier:
            ke, ve = block(i - 1)
        res = []
        for qh in qs:
            zd, cumd = _sb_scores(qh, kd, tri, diag_visible)
            c = cumd[:, 0:1]
            acc = _dot(_sb_weights(zd, cumd, 0.0, diag_visible), vd)
            if with_earlier:
                ze, cume = _sb_scores(qh, ke, tri, None)
                acc = acc + _dot(_sb_weights(ze, cume, c, None), ve)
                c = c + cume[:, 0:1]
            res += [c, acc]
        return tuple(res)

    c0, a0, c1, a1 = lax.cond(i > 0, lambda: first_blocks(True), lambda: first_blocks(False))

    def visit(st):
        j, _, c0, a0, c1, a1 = st
        kb, vb = block(j)
        out = []
        for qh, c, acc in ((qs[0], c0, a0), (qs[1], c1, a1)):
            z, cum = _sb_scores(qh, kb, tri, None)
            out += [c + cum[:, 0:1], acc + _dot(_sb_weights(z, cum, c, None), vb)]
        return (j - 1, _unfinished(out[0], out[2])) + tuple(out)

    st = lax.while_loop(_more, visit, (i - 2, _unfinished(c0, c1), c0, a0, c1, a1))
    o_ref[...] = jnp.where(in_head[0], st[3], st[5]).astype(o_ref.dtype)


def _attention(q, k, v, *, bq):
    bsz, L, _ = q.shape
    grid = (bsz, SB_WIDTH // LANES, L // bq)
    qspec = pl.BlockSpec((None, bq, LANES), lambda b, p, i: (b, i, p))
    kvspec = pl.BlockSpec((None, L, LANES), lambda b, p, i: (b, 0, p))
    return pl.pallas_call(
        functools.partial(_attn_kernel, bq=bq),
        grid=grid,
        in_specs=[qspec, kvspec, kvspec],
        out_specs=qspec,
        out_shape=jax.ShapeDtypeStruct((bsz, L, SB_WIDTH), BF16),
        compiler_params=pltpu.CompilerParams(
            dimension_semantics=("parallel", "parallel", "parallel"), vmem_limit_bytes=VMEM_LIMIT),
        name="sb_attention",
    )(q, k, v)


def _attn_cached_kernel(q_ref, k_ref, v_ref, kl_ref, vl_ref, kc_hbm, vc_hbm, o_ref, kbuf, vbuf, sem,
                        *, layer, lq, bk, n_past):
    b = pl.program_id(0)
    hd = SB_HEAD_DIM
    q = q_ref[...]
    kn = k_ref[...].astype(BF16)
    vn = v_ref[...].astype(BF16)
    head = lambda x, h: x[:, h * hd:(h + 1) * hd]
    qs = [head(q, h) for h in range(SB_HEADS)]
    rows = SB_HEADS * lq

    def nt(a, bmat):
        return lax.dot_general(a, bmat, (((1,), (1,)), ((), ())), preferred_element_type=F32)

    def stacked_scores(z, tri, visible):
        sp = _softplus(z)
        if visible is not None:
            sp = jnp.where(visible, sp, 0.0)
        hi = sp.astype(BF16)
        lo = (sp - hi.astype(F32)).astype(BF16)
        return _dot(hi, tri) + _dot(lo, tri)

    r = lax.broadcasted_iota(jnp.int32, (rows, lq), 0) % lq
    s = lax.broadcasted_iota(jnp.int32, (rows, lq), 1)
    visible = s < r
    tr = lax.broadcasted_iota(jnp.int32, (lq, lq), 0)
    tc = lax.broadcasted_iota(jnp.int32, (lq, lq), 1)
    zd = jnp.concatenate([nt(qs[h], head(kn, h)) for h in range(SB_HEADS)], axis=0)
    cumd = stacked_scores(zd, (tr >= tc).astype(BF16), visible)
    wd = _sb_weights(zd, cumd, 0.0, visible)
    accs = [_dot(wd[h * lq:(h + 1) * lq], head(vn, h)) for h in range(SB_HEADS)]
    c = cumd[:, 0:1]

    pr = lax.broadcasted_iota(jnp.int32, (bk, bk), 0)
    pc = lax.broadcasted_iota(jnp.int32, (bk, bk), 1)
    tri_p = (pr >= pc).astype(BF16)

    def cache_block(kref, vref, c, accs):
        z = jnp.concatenate([nt(qs[h], kref[:, h, :].astype(BF16)) for h in range(SB_HEADS)], axis=0)
        cum = stacked_scores(z, tri_p, None)
        w = _sb_weights(z, cum, c, None)
        accs = [accs[h] + _dot(w[h * lq:(h + 1) * lq], vref[:, h, :].astype(BF16)) for h in range(SB_HEADS)]
        return c + cum[:, 0:1], accs

    c, accs = cache_block(kl_ref, vl_ref, c, accs)

    def fetch(j):
        s0 = pl.multiple_of(j * bk, bk)
        return (pltpu.make_async_copy(kc_hbm.at[layer, b, pl.ds(s0, bk)], kbuf, sem.at[0]),
                pltpu.make_async_copy(vc_hbm.at[layer, b, pl.ds(s0, bk)], vbuf, sem.at[1]))

    def visit(st):
        j, c, accs = st[0], st[2], list(st[3:])
        copies = fetch(j)
        for cp in copies:
            cp.start()
        for cp in copies:
            cp.wait()
        c, accs = cache_block(kbuf, vbuf, c, accs)
        return (j - 1, _unfinished(c), c) + tuple(accs)

    st = lax.while_loop(_more, visit, (n_past - 2, _unfinished(c), c) + tuple(accs))
    o_ref[...] = jnp.concatenate(st[3:], axis=1).astype(o_ref.dtype)


def _attention_cached(q, k, v, cache_k, cache_v, layer, *, bk=256):
    bsz, lq, _ = q.shape
    past = cache_k.shape[2]
    n_past = past // bk
    tok = pl.BlockSpec((None, lq, SB_WIDTH), lambda b: (b, 0, 0))
    last = pl.BlockSpec((None, None, bk, SB_HEADS, SB_HEAD_DIM), lambda b: (layer, b, n_past - 1, 0, 0))
    hbm = pl.BlockSpec(memory_space=pl.ANY)
    return pl.pallas_call(
        functools.partial(_attn_cached_kernel, layer=layer, lq=lq, bk=bk, n_past=n_past),
        grid=(bsz,),
        in_specs=[tok, tok, tok, last, last, hbm, hbm],
        out_specs=tok,
        out_shape=jax.ShapeDtypeStruct((bsz, lq, SB_WIDTH), BF16),
        scratch_shapes=[pltpu.VMEM((bk, SB_HEADS, SB_HEAD_DIM), F32), pltpu.VMEM((bk, SB_HEADS, SB_HEAD_DIM), F32),
                        pltpu.SemaphoreType.DMA((2,))],
        compiler_params=pltpu.CompilerParams(
            dimension_semantics=("parallel",), vmem_limit_bytes=VMEM_LIMIT),
        name="sb_attention_cached",
    )(q, k, v, cache_k, cache_v, cache_k, cache_v)


def _gelu_tanh(x):
    return 0.5 * x * (1.0 + jnp.tanh(math.sqrt(2.0 / math.pi) * (x + 0.044715 * (x * x * x))))


def _ssm_kernel(u_ref, s0re_ref, s0im_ref, lbre_ref, lbim_ref, bre_ref, bim_ref, cre_ref, cimn_ref,
                d_ref, wglu_ref, o_ref, sfre_ref, sfim_ref, sre, sim, st_re, st_im, *, bsz, tc, lw):
    step = pl.program_id(0)

    @pl.when(step == 0)
    def _():
        st_re[...] = s0re_ref[...]
        st_im[...] = s0im_ref[...]

    u = u_ref[...]
    ub = u.astype(BF16)
    for m in range(SSM_SLABS):
        um = ub[:, m * LANES:(m + 1) * LANES]
        sre[:, m * SLAB_STATE:(m + 1) * SLAB_STATE] = _dot(um, bre_ref[m])
        sim[:, m * SLAB_STATE:(m + 1) * SLAB_STATE] = _dot(um, bim_ref[m])

    for lc in range(SSM_LANES // lw):
        lanes = slice(lc * lw, (lc + 1) * lw)
        lr = jnp.broadcast_to(lbre_ref[:, lanes], (8, lw))
        li = jnp.broadcast_to(lbim_ref[:, lanes], (8, lw))
        for sb in range(bsz // 8):
            subl = slice(sb * 8, (sb + 1) * 8)

            def body(t, carry):
                sr, si = carry
                r0 = pl.multiple_of(t * bsz + sb * 8, 8)
                nr = lr * sr - li * si + sre[pl.ds(r0, 8), lanes]
                ni = lr * si + li * sr + sim[pl.ds(r0, 8), lanes]
                sre[pl.ds(r0, 8), lanes] = nr
                sim[pl.ds(r0, 8), lanes] = ni
                return nr, ni

            sr, si = lax.fori_loop(0, tc, body, (st_re[subl, lanes], st_im[subl, lanes]), unroll=4)
            st_re[subl, lanes] = sr
            st_im[subl, lanes] = si

    ys = []
    for m in range(SSM_SLABS):
        slab = slice(m * SLAB_STATE, (m + 1) * SLAB_STATE)
        ys.append(_dot(sre[:, slab].astype(BF16), cre_ref[m]) + _dot(sim[:, slab].astype(BF16), cimn_ref[m]))
    y = jnp.concatenate(ys, axis=1) + d_ref[...] * u
    y = _gelu_tanh(y)
    o_ref[...] = (y * _sigmoid(_dot(y.astype(BF16), wglu_ref[...]))).astype(o_ref.dtype)

    @pl.when(step == pl.num_programs(0) - 1)
    def _():
        sfre_ref[...] = st_re[...]
        sfim_ref[...] = st_im[...]


def _ssm(u_tm, s0_re, s0_im, prm, *, bsz, tc, lw=512):
    rows = u_tm.shape[0]
    blk = tc * bsz
    row_spec = pl.BlockSpec((blk, SSM_WIDTH), lambda s: (s, 0))
    st_shape = (bsz, SSM_LANES)
    return pl.pallas_call(
        functools.partial(_ssm_kernel, bsz=bsz, tc=tc, lw=lw),
        grid=(rows // blk,),
        in_specs=[row_spec, _const_spec(st_shape), _const_spec(st_shape),
                  _const_spec((1, SSM_LANES)), _const_spec((1, SSM_LANES)),
                  _const_spec((SSM_SLABS, LANES, SLAB_STATE)), _const_spec((SSM_SLABS, LANES, SLAB_STATE)),
                  _const_spec((SSM_SLABS, SLAB_STATE, LANES)), _const_spec((SSM_SLABS, SLAB_STATE, LANES)),
                  _const_spec((1, SSM_WIDTH)), _const_spec((SSM_WIDTH, SSM_WIDTH))],
        out_specs=[row_spec, _const_spec(st_shape), _const_spec(st_shape)],
        out_shape=[jax.ShapeDtypeStruct((rows, SSM_WIDTH), BF16),
                   jax.ShapeDtypeStruct(st_shape, F32), jax.ShapeDtypeStruct(st_shape, F32)],
        scratch_shapes=[pltpu.VMEM((blk, SSM_LANES), F32), pltpu.VMEM((blk, SSM_LANES), F32),
                        pltpu.VMEM(st_shape, F32), pltpu.VMEM(st_shape, F32)],
        compiler_params=pltpu.CompilerParams(
            dimension_semantics=("arbitrary",), vmem_limit_bytes=VMEM_LIMIT),
        name="s5_scan",
    )(u_tm, s0_re, s0_im, prm["lb_re"], prm["lb_im"], prm["bd_re"], prm["bd_im"],
      prm["cd_re"], prm["cd_im_neg"], prm["d"], prm["w_glu"])


def _ssm_params(a_re, a_im, log_dt, b_re, b_im, c_re, c_im, d, w_glu):
    dt = jnp.exp(log_dt)[:, None]
    mag = jnp.exp(a_re * dt)
    lb_re = mag * jnp.cos(a_im * dt)
    lb_im = mag * jnp.sin(a_im * dt)
    den = a_re * a_re + a_im * a_im
    nr, ni = lb_re - 1.0, lb_im
    f_re = (nr * a_re + ni * a_im) / den
    f_im = (ni * a_re - nr * a_im) / den
    bb_re = f_re[:, :, None] * b_re - f_im[:, :, None] * b_im
    bb_im = f_re[:, :, None] * b_im + f_im[:, :, None] * b_re
    gps = SSM_GROUPS // SSM_SLABS
    eye = jnp.eye(gps, dtype=F32)

    def b_slabs(bb):
        t = bb.transpose(0, 2, 1).reshape(SSM_SLABS, gps, SSM_GROUP, SSM_STATE)
        return jnp.einsum("mgcp,gh->mgchp", t, eye).reshape(SSM_SLABS, LANES, SLAB_STATE).astype(BF16)

    def c_slabs(c):
        t = c.transpose(0, 2, 1).reshape(SSM_SLABS, gps, SSM_STATE, SSM_GROUP)
        return jnp.einsum("mgpc,gh->mgphc", t, eye).reshape(SSM_SLABS, SLAB_STATE, LANES).astype(BF16)

    return dict(lb_re=lb_re.reshape(1, SSM_LANES), lb_im=lb_im.reshape(1, SSM_LANES),
                bd_re=b_slabs(bb_re), bd_im=b_slabs(bb_im),
                cd_re=c_slabs(c_re), cd_im_neg=c_slabs(-c_im),
                d=d.reshape(1, SSM_WIDTH), w_glu=w_glu.astype(BF16))


def _post_kernel(x_ref, oa_ref, os_ref, ga_ref, gs_ref, p_ref,
                 wba_ref, wbs_ref, wout_ref, wg_ref, wu_ref, wd_ref, wpg_ref, wpp_ref,
                 n_mix_post, n_ffn_pre, n_ffn_post, n_ple_pre, n_ple_post,
                 y_ref, *, nb, tm, ff_chunk):
    rows = nb * tm
    x = x_ref[...].reshape(rows, D_MODEL)
    oa = oa_ref[...].reshape(rows, SB_WIDTH)
    if nb == 1:
        os_ = os_ref[...]
    else:
        os_ = jnp.concatenate([os_ref[:, b * SSM_WIDTH:(b + 1) * SSM_WIDTH] for b in range(nb)], axis=0)
    merged = (_sigmoid(ga_ref[...].reshape(rows, D_MODEL)) * _dot(oa, wba_ref[...])
              + _sigmoid(gs_ref[...].reshape(rows, D_MODEL)) * _dot(os_, wbs_ref[...]))
    x = x + _rms(_dot(merged.astype(BF16), wout_ref[...]), n_mix_post[...])

    f = _rms(x, n_ffn_pre[...]).astype(BF16)
    ff = jnp.zeros((rows, D_MODEL), F32)
    for c in range(D_FF // ff_chunk):
        cols = slice(c * ff_chunk, (c + 1) * ff_chunk)
        g = _dot(f, wg_ref[:, cols])
        a = (g * _sigmoid(g)) * _dot(f, wu_ref[:, cols])
        ff = ff + _dot(a.astype(BF16), wd_ref[cols, :])
    x = x + _rms(ff, n_ffn_post[...])

    gate = _sigmoid(_dot(_rms(x, n_ple_pre[...]).astype(BF16), wpg_ref[...]))
    pe = gate * _dot(p_ref[...].reshape(rows, PLE_DIM).astype(BF16), wpp_ref[...])
    y_ref[...] = (x + _rms(pe, n_ple_post[...])).reshape(nb, tm, D_MODEL)


def _post(x, o_attn, o_ssm_tm, g_attn, g_ssm, p, W, *, nb, tm, ff_chunk=256):
    bsz, L, _ = x.shape
    grid = (bsz // nb, L // tm)
    tok = lambda width: pl.BlockSpec((nb, tm, width), lambda b, i: (b, i, 0))
    vec = _const_spec((1, D_MODEL))
    return pl.pallas_call(
        functools.partial(_post_kernel, nb=nb, tm=tm, ff_chunk=ff_chunk),
        grid=grid,
        in_specs=[tok(D_MODEL), tok(SB_WIDTH),
                  pl.BlockSpec((tm, nb * SSM_WIDTH), lambda b, i: (i, b)),
                  tok(D_MODEL), tok(D_MODEL), tok(PLE_DIM),
                  _const_spec((SB_WIDTH, D_MODEL)), _const_spec((SSM_WIDTH, D_MODEL)),
                  _const_spec((D_MODEL, D_MODEL)),
                  _const_spec((D_MODEL, D_FF)), _const_spec((D_MODEL, D_FF)), _const_spec((D_FF, D_MODEL)),
                  _const_spec((D_MODEL, D_MODEL)), _const_spec((PLE_DIM, D_MODEL)),
                  vec, vec, vec, vec, vec],
        out_specs=tok(D_MODEL),
        out_shape=jax.ShapeDtypeStruct((bsz, L, D_MODEL), F32),
        compiler_params=pltpu.CompilerParams(
            dimension_semantics=("parallel", "parallel"), vmem_limit_bytes=VMEM_LIMIT),
        name="post",
    )(x, o_attn, o_ssm_tm, g_attn, g_ssm, p,
      W["w_branch_attn"], W["w_branch_ssm"], W["w_out"], W["w_ffn_gate"], W["w_ffn_up"], W["w_ffn_down"],
      W["w_ple_gate"], W["w_ple_proj"],
      W["norm_mix_post"], W["norm_ffn_pre"], W["norm_ffn_post"], W["norm_ple_pre"], W["norm_ple_post"])


def _layer(x, p, cache, s_re0, s_im0, W, *, nb, tm, bq, tc, post_nb, post_tm):
    bsz, L, _ = x.shape
    q, k, v, u_tm, g_attn, g_ssm = _inproj(x, W["norm_mix_pre"], W["w_in"], nb=nb, tm=tm)
    if cache is None:
        o_attn = _attention(q, k, v, bq=bq)
    else:
        o_attn = _attention_cached(q, k, v, *cache)
    o_ssm_tm, s_re, s_im = _ssm(u_tm.reshape(L * bsz, SSM_WIDTH), s_re0.reshape(bsz, SSM_LANES),
                                s_im0.reshape(bsz, SSM_LANES), W["ssm"], bsz=bsz, tc=tc)
    y = _post(x, o_attn, o_ssm_tm.reshape(L, bsz * SSM_WIDTH), g_attn, g_ssm, p, W, nb=post_nb, tm=post_tm)
    heads = (bsz, L, SB_HEADS, SB_HEAD_DIM)
    state = (bsz, SSM_GROUPS, SSM_STATE)
    return y, k.reshape(heads), v.reshape(heads), s_re.reshape(state), s_im.reshape(state)


def kernel(x_prompt, x_sample, cache_k, cache_v, state_ssm_re, state_ssm_im, p_prompt, p_sample, norm_mix_pre, norm_mix_post, w_in, ssm_a_re, ssm_a_im, ssm_log_dt, ssm_b_re, ssm_b_im, ssm_c_re, ssm_c_im, ssm_d, w_glu, w_branch_attn, w_branch_ssm, w_out, norm_ffn_pre, norm_ffn_post, w_ffn_gate, w_ffn_up, w_ffn_down, norm_ple_pre, norm_ple_post, w_ple_gate, w_ple_proj):
    depth = w_in.shape[0]
    yp, ys = x_prompt, x_sample
    outs = [[] for _ in range(8)]
    for i in range(depth):
        W = dict(
            norm_mix_pre=norm_mix_pre[i][None], norm_mix_post=norm_mix_post[i][None],
            norm_ffn_pre=norm_ffn_pre[i][None], norm_ffn_post=norm_ffn_post[i][None],
            norm_ple_pre=norm_ple_pre[i][None], norm_ple_post=norm_ple_post[i][None],
            w_in=w_in[i].astype(BF16), w_branch_attn=w_branch_attn[i].astype(BF16),
            w_branch_ssm=w_branch_ssm[i].astype(BF16), w_out=w_out[i].astype(BF16),
            w_ffn_gate=w_ffn_gate[i].astype(BF16), w_ffn_up=w_ffn_up[i].astype(BF16),
            w_ffn_down=w_ffn_down[i].astype(BF16), w_ple_gate=w_ple_gate[i].astype(BF16),
            w_ple_proj=w_ple_proj[i].astype(BF16),
            ssm=_ssm_params(ssm_a_re[i], ssm_a_im[i], ssm_log_dt[i], ssm_b_re[i], ssm_b_im[i],
                            ssm_c_re[i], ssm_c_im[i], ssm_d[i], w_glu[i]))
        bp, lp = yp.shape[0], yp.shape[1]
        bs, ls = ys.shape[0], ys.shape[1]
        zero_state = jnp.zeros((bp, SSM_GROUPS, SSM_STATE), F32)
        yp, kp, vp, srp, sip = _layer(
            yp, p_prompt[i], None, zero_state, zero_state, W,
            nb=1, tm=min(512, lp), bq=min(256, lp), tc=512 // bp, post_nb=1, post_tm=min(256, lp))
        ys, kn, vn, srs, sis = _layer(
            ys, p_sample[i], (cache_k, cache_v, i), state_ssm_re[i], state_ssm_im[i], W,
            nb=512 // ls, tm=ls, bq=ls, tc=512 // bs, post_nb=256 // ls, post_tm=ls)
        for lst, val in zip(outs, (kp, vp, srp, sip, kn, vn, srs, sis)):
            lst.append(val)
    return (yp, ys) + tuple(jnp.stack(o) for o in outs)
```

```python
import functools
import math

import jax
import jax.numpy as jnp
from jax import lax
from jax.experimental import pallas as pl
from jax.experimental.pallas import tpu as pltpu

F32 = jnp.float32
BF16 = jnp.bfloat16

D_MODEL = 1024
PLE_DIM = 256
SB_WIDTH = 512
SB_HEAD_DIM = 64
SB_HEADS = 8
SSM_WIDTH = 512
SSM_GROUP = 16
SSM_GROUPS = 32
SSM_STATE = 64
SSM_LANES = SSM_GROUPS * SSM_STATE
D_FF = 2816
IN_WIDTH = 3 * SB_WIDTH + SSM_WIDTH + 2 * D_MODEL
RMS_EPS = 1e-6

LANES = 128
HEADS_PER_BLOCK = LANES // SB_HEAD_DIM
SSM_SLABS = SSM_WIDTH // LANES
SLAB_STATE = SSM_LANES // SSM_SLABS
VMEM_LIMIT = 56 * 1024 * 1024


def _const_spec(shape):
    nd = len(shape)
    return pl.BlockSpec(shape, lambda *_: (0,) * nd, pipeline_mode=pl.Buffered(1))


def _rms(x, gain):
    ms = jnp.mean(x * x, axis=-1, keepdims=True)
    return x * lax.rsqrt(ms + RMS_EPS) * gain


def _sigmoid(x):
    return 1.0 / (1.0 + jnp.exp(-x))


def _dot(a, b):
    return jnp.dot(a, b, preferred_element_type=F32)


def _dot_nt(a, b):
    return lax.dot_general(a, b, (((1,), (1,)), ((), ())), preferred_element_type=F32)


def _inproj_kernel(*refs, nb, tm, kv_t):
    if kv_t:
        x_ref, gain_ref, w_ref, wkvt_ref, q_ref, k_ref, v_ref, u_ref, ga_ref, gs_ref = refs
    else:
        x_ref, gain_ref, w_ref, q_ref, k_ref, v_ref, u_ref, ga_ref, gs_ref = refs
    x = x_ref[...].reshape(nb * tm, D_MODEL)
    h = _rms(x, gain_ref[...]).astype(BF16)

    def proj(lo, width):
        return _dot(h, w_ref[:, lo:lo + width])

    q_ref[...] = (proj(0, SB_WIDTH) * (SB_HEAD_DIM ** -0.5)).astype(BF16).reshape(nb, tm, SB_WIDTH)
    if kv_t:
        k_ref[...] = _dot_nt(wkvt_ref[0:SB_WIDTH, :], h).reshape(SB_HEADS, SB_HEAD_DIM, tm)
        v_ref[...] = _dot_nt(wkvt_ref[SB_WIDTH:2 * SB_WIDTH, :], h).reshape(SB_HEADS, SB_HEAD_DIM, tm)
    else:
        k_ref[...] = proj(SB_WIDTH, SB_WIDTH).reshape(nb, tm, SB_WIDTH)
        v_ref[...] = proj(2 * SB_WIDTH, SB_WIDTH).reshape(nb, tm, SB_WIDTH)
    u = proj(3 * SB_WIDTH, SSM_WIDTH)
    for b in range(nb):
        u_ref[:, b * SSM_WIDTH:(b + 1) * SSM_WIDTH] = u[b * tm:(b + 1) * tm, :]
    off = 3 * SB_WIDTH + SSM_WIDTH
    ga_ref[...] = proj(off, D_MODEL).reshape(nb, tm, D_MODEL)
    gs_ref[...] = proj(off + D_MODEL, D_MODEL).reshape(nb, tm, D_MODEL)


def _inproj(x, gain, w_in, w_kv_t, *, nb, tm, kv_t):
    bsz, L, _ = x.shape
    grid = (bsz // nb, L // tm)
    tok = lambda width: pl.BlockSpec((nb, tm, width), lambda b, i: (b, i, 0))
    in_specs = [tok(D_MODEL), _const_spec((1, D_MODEL)), _const_spec((D_MODEL, IN_WIDTH))]
    args = [x, gain, w_in]
    if kv_t:
        assert nb == 1
        in_specs.append(_const_spec((2 * SB_WIDTH, D_MODEL)))
        args.append(w_kv_t)
        kv_spec = pl.BlockSpec((None, SB_HEADS, SB_HEAD_DIM, tm), lambda b, i: (b, 0, 0, i))
        kv_shape = jax.ShapeDtypeStruct((bsz, SB_HEADS, SB_HEAD_DIM, L), F32)
    else:
        kv_spec = tok(SB_WIDTH)
        kv_shape = jax.ShapeDtypeStruct((bsz, L, SB_WIDTH), F32)
    return pl.pallas_call(
        functools.partial(_inproj_kernel, nb=nb, tm=tm, kv_t=kv_t),
        grid=grid,
        in_specs=in_specs,
        out_specs=[tok(SB_WIDTH), kv_spec, kv_spec,
                   pl.BlockSpec((tm, nb * SSM_WIDTH), lambda b, i: (i, b)),
                   tok(D_MODEL), tok(D_MODEL)],
        out_shape=[jax.ShapeDtypeStruct((bsz, L, SB_WIDTH), BF16), kv_shape, kv_shape,
                   jax.ShapeDtypeStruct((L, bsz * SSM_WIDTH), F32),
                   jax.ShapeDtypeStruct((bsz, L, D_MODEL), F32),
                   jax.ShapeDtypeStruct((bsz, L, D_MODEL), F32)],
        compiler_params=pltpu.CompilerParams(
            dimension_semantics=("parallel", "parallel"), vmem_limit_bytes=VMEM_LIMIT),
        name="inproj",
    )(*args)


def _softplus(z):
    return jnp.maximum(z, 0.0) + jnp.log(1.0 + jnp.exp(-jnp.abs(z)))


SKIP_LOG = 105.0


def _sb_scores(qh, kt, tri, visible):
    z = _dot(qh, kt)
    sp = _softplus(z)
    if visible is not None:
        sp = jnp.where(visible, sp, 0.0)
    hi = sp.astype(BF16)
    lo = (sp - hi.astype(F32)).astype(BF16)
    return z, _dot(hi, tri) + _dot(lo, tri)


def _sb_weights(z, cum, c, visible):
    w = jnp.exp(z - cum - c)
    if visible is not None:
        w = jnp.where(visible, w, 0.0)
    return w.astype(BF16)


def _unfinished(*cs):
    m = jnp.min(cs[0])
    for c in cs[1:]:
        m = jnp.minimum(m, jnp.min(c))
    return (m < SKIP_LOG).astype(jnp.int32)


def _more(st):
    return (st[0] >= 0) & (st[1] > 0)


def _attn_kernel(q_ref, k_ref, v_ref, o_ref, *, bq):
    i = pl.program_id(2)
    q = q_ref[...]
    lane = lax.broadcasted_iota(jnp.int32, (bq, LANES), 1)
    row = lax.broadcasted_iota(jnp.int32, (bq, bq), 0)
    col = lax.broadcasted_iota(jnp.int32, (bq, bq), 1)
    tri = (row >= col).astype(BF16)
    diag_visible = col < row

    in_head = [(lane >= h * SB_HEAD_DIM) & (lane < (h + 1) * SB_HEAD_DIM) for h in range(HEADS_PER_BLOCK)]
    qs = [jnp.where(m, q, jnp.zeros_like(q)) for m in in_head]

    def block(j):
        s0 = pl.multiple_of(j * bq, bq)
        return (k_ref[:, :, pl.ds(s0, bq)].reshape(LANES, bq).astype(BF16),
                v_ref[:, :, pl.ds(s0, bq)].reshape(LANES, bq).astype(BF16))

    def first_blocks(with_earlier):
        kd, vd = block(i)
        if with_earlier:
            ke, ve = block(i - 1)
        res = []
        for qh in qs:
            zd, cumd = _sb_scores(qh, kd, tri, diag_visible)
            c = cumd[:, 0:1]
            acc = _dot_nt(_sb_weights(zd, cumd, 0.0, diag_visible), vd)
            if with_earlier:
                ze, cume = _sb_scores(qh, ke, tri, None)
                acc = acc + _dot_nt(_sb_weights(ze, cume, c, None), ve)
                c = c + cume[:, 0:1]
            res += [c, acc]
        return tuple(res)

    c0, a0, c1, a1 = lax.cond(i > 0, lambda: first_blocks(True), lambda: first_blocks(False))

    def visit(st):
        j, _, c0, a0, c1, a1 = st
        kb, vb = block(j)
        out = []
        for qh, c, acc in ((qs[0], c0, a0), (qs[1], c1, a1)):
            z, cum = _sb_scores(qh, kb, tri, None)
            out += [c + cum[:, 0:1], acc + _dot_nt(_sb_weights(z, cum, c, None), vb)]
        return (j - 1, _unfinished(out[0], out[2])) + tuple(out)

    st = lax.while_loop(_more, visit, (i - 2, _unfinished(c0, c1), c0, a0, c1, a1))
    o_ref[...] = jnp.where(in_head[0], st[3], st[5]).astype(o_ref.dtype)


def _attention(q, k_t, v_t, *, bq):
    bsz, L, _ = q.shape
    grid = (bsz, SB_WIDTH // LANES, L // bq)
    qspec = pl.BlockSpec((None, bq, LANES), lambda b, p, i: (b, i, p))
    kvspec = pl.BlockSpec((None, HEADS_PER_BLOCK, SB_HEAD_DIM, L), lambda b, p, i: (b, p, 0, 0))
    return pl.pallas_call(
        functools.partial(_attn_kernel, bq=bq),
        grid=grid,
        in_specs=[qspec, kvspec, kvspec],
        out_specs=qspec,
        out_shape=jax.ShapeDtypeStruct((bsz, L, SB_WIDTH), BF16),
        compiler_params=pltpu.CompilerParams(
            dimension_semantics=("parallel", "parallel", "parallel"), vmem_limit_bytes=VMEM_LIMIT),
        name="sb_attention",
    )(q, k_t, v_t)


def _attn_cached_kernel(q_ref, k_ref, v_ref, kl_ref, vl_ref, kc_hbm, vc_hbm, o_ref, kbuf, vbuf, sem,
                        *, layer, lq, bk, n_past):
    b = pl.program_id(0)
    hd = SB_HEAD_DIM
    q = q_ref[...]
    kn = k_ref[...].astype(BF16)
    vn = v_ref[...].astype(BF16)
    head = lambda x, h: x[:, h * hd:(h + 1) * hd]
    qs = [head(q, h) for h in range(SB_HEADS)]
    rows = SB_HEADS * lq

    def stacked_scores(z, tri, visible):
        sp = _softplus(z)
        if visible is not None:
            sp = jnp.where(visible, sp, 0.0)
        hi = sp.astype(BF16)
        lo = (sp - hi.astype(F32)).astype(BF16)
        return _dot(hi, tri) + _dot(lo, tri)

    r = lax.broadcasted_iota(jnp.int32, (rows, lq), 0) % lq
    s = lax.broadcasted_iota(jnp.int32, (rows, lq), 1)
    visible = s < r
    tr = lax.broadcasted_iota(jnp.int32, (lq, lq), 0)
    tc = lax.broadcasted_iota(jnp.int32, (lq, lq), 1)
    zd = jnp.concatenate([_dot_nt(qs[h], head(kn, h)) for h in range(SB_HEADS)], axis=0)
    cumd = stacked_scores(zd, (tr >= tc).astype(BF16), visible)
    wd = _sb_weights(zd, cumd, 0.0, visible)
    accs = [_dot(wd[h * lq:(h + 1) * lq], head(vn, h)) for h in range(SB_HEADS)]
    c = cumd[:, 0:1]

    pr = lax.broadcasted_iota(jnp.int32, (bk, bk), 0)
    pc = lax.broadcasted_iota(jnp.int32, (bk, bk), 1)
    tri_p = (pr >= pc).astype(BF16)

    def cache_block(kref, vref, c, accs):
        z = jnp.concatenate([_dot(qs[h], kref[h].astype(BF16)) for h in range(SB_HEADS)], axis=0)
        cum = stacked_scores(z, tri_p, None)
        w = _sb_weights(z, cum, c, None)
        accs = [accs[h] + _dot_nt(w[h * lq:(h + 1) * lq], vref[h].astype(BF16)) for h in range(SB_HEADS)]
        return c + cum[:, 0:1], accs

    c, accs = cache_block(kl_ref, vl_ref, c, accs)

    def fetch(j):
        s0 = pl.multiple_of(j * bk, bk)
        return (pltpu.make_async_copy(kc_hbm.at[layer, b, :, :, pl.ds(s0, bk)], kbuf, sem.at[0]),
                pltpu.make_async_copy(vc_hbm.at[layer, b, :, :, pl.ds(s0, bk)], vbuf, sem.at[1]))

    def visit(st):
        j, c, accs = st[0], st[2], list(st[3:])
        copies = fetch(j)
        for cp in copies:
            cp.start()
        for cp in copies:
            cp.wait()
        c, accs = cache_block(kbuf, vbuf, c, accs)
        return (j - 1, _unfinished(c), c) + tuple(accs)

    st = lax.while_loop(_more, visit, (n_past - 2, _unfinished(c), c) + tuple(accs))
    o_ref[...] = jnp.concatenate(st[3:], axis=1).astype(o_ref.dtype)


def _attention_cached(q, k, v, cache_kt, cache_vt, layer, *, bk=256):
    bsz, lq, _ = q.shape
    past = cache_kt.shape[4]
    n_past = past // bk
    tok = pl.BlockSpec((None, lq, SB_WIDTH), lambda b: (b, 0, 0))
    last = pl.BlockSpec((None, None, SB_HEADS, SB_HEAD_DIM, bk), lambda b: (layer, b, 0, 0, n_past - 1))
    hbm = pl.BlockSpec(memory_space=pl.ANY)
    return pl.pallas_call(
        functools.partial(_attn_cached_kernel, layer=layer, lq=lq, bk=bk, n_past=n_past),
        grid=(bsz,),
        in_specs=[tok, tok, tok, last, last, hbm, hbm],
        out_specs=tok,
        out_shape=jax.ShapeDtypeStruct((bsz, lq, SB_WIDTH), BF16),
        scratch_shapes=[pltpu.VMEM((SB_HEADS, SB_HEAD_DIM, bk), F32), pltpu.VMEM((SB_HEADS, SB_HEAD_DIM, bk), F32),
                        pltpu.SemaphoreType.DMA((2,))],
        compiler_params=pltpu.CompilerParams(
            dimension_semantics=("parallel",), vmem_limit_bytes=VMEM_LIMIT),
        name="sb_attention_cached",
    )(q, k, v, cache_kt, cache_vt, cache_kt, cache_vt)


def _gelu_tanh(x):
    return 0.5 * x * (1.0 + jnp.tanh(math.sqrt(2.0 / math.pi) * (x + 0.044715 * (x * x * x))))


def _ssm_kernel(u_ref, s0re_ref, s0im_ref, lbre_ref, lbim_ref, bre_ref, bim_ref, cre_ref, cimn_ref,
                d_ref, wglu_ref, o_ref, sfre_ref, sfim_ref, sre, sim, st_re, st_im, *, bsz, tc, lw):
    step = pl.program_id(0)

    @pl.when(step == 0)
    def _():
        st_re[...] = s0re_ref[...]
        st_im[...] = s0im_ref[...]

    u = u_ref[...]
    ub = u.astype(BF16)
    for m in range(SSM_SLABS):
        um = ub[:, m * LANES:(m + 1) * LANES]
        sre[:, m * SLAB_STATE:(m + 1) * SLAB_STATE] = _dot(um, bre_ref[m])
        sim[:, m * SLAB_STATE:(m + 1) * SLAB_STATE] = _dot(um, bim_ref[m])

    for lc in range(SSM_LANES // lw):
        lanes = slice(lc * lw, (lc + 1) * lw)
        lr = jnp.broadcast_to(lbre_ref[:, lanes], (8, lw))
        li = jnp.broadcast_to(lbim_ref[:, lanes], (8, lw))
        for sb in range(bsz // 8):
            subl = slice(sb * 8, (sb + 1) * 8)

            def body(t, carry):
                sr, si = carry
                r0 = pl.multiple_of(t * bsz + sb * 8, 8)
                nr = lr * sr - li * si + sre[pl.ds(r0, 8), lanes]
                ni = lr * si + li * sr + sim[pl.ds(r0, 8), lanes]
                sre[pl.ds(r0, 8), lanes] = nr
                sim[pl.ds(r0, 8), lanes] = ni
                return nr, ni

            sr, si = lax.fori_loop(0, tc, body, (st_re[subl, lanes], st_im[subl, lanes]), unroll=4)
            st_re[subl, lanes] = sr
            st_im[subl, lanes] = si

    ys = []
    for m in range(SSM_SLABS):
        slab = slice(m * SLAB_STATE, (m + 1) * SLAB_STATE)
        ys.append(_dot(sre[:, slab].astype(BF16), cre_ref[m]) + _dot(sim[:, slab].astype(BF16), cimn_ref[m]))
    y = jnp.concatenate(ys, axis=1) + d_ref[...] * u
    y = _gelu_tanh(y)
    o_ref[...] = (y * _sigmoid(_dot(y.astype(BF16), wglu_ref[...]))).astype(o_ref.dtype)

    @pl.when(step == pl.num_programs(0) - 1)
    def _():
        sfre_ref[...] = st_re[...]
        sfim_ref[...] = st_im[...]


def _ssm(u_tm, s0_re, s0_im, prm, *, bsz, tc, lw=512):
    rows = u_tm.shape[0]
    blk = tc * bsz
    row_spec = pl.BlockSpec((blk, SSM_WIDTH), lambda s: (s, 0))
    st_shape = (bsz, SSM_LANES)
    return pl.pallas_call(
        functools.partial(_ssm_kernel, bsz=bsz, tc=tc, lw=lw),
        grid=(rows // blk,),
        in_specs=[row_spec, _const_spec(st_shape), _const_spec(st_shape),
                  _const_spec((1, SSM_LANES)), _const_spec((1, SSM_LANES)),
                  _const_spec((SSM_SLABS, LANES, SLAB_STATE)), _const_spec((SSM_SLABS, LANES, SLAB_STATE)),
                  _const_spec((SSM_SLABS, SLAB_STATE, LANES)), _const_spec((SSM_SLABS, SLAB_STATE, LANES)),
                  _const_spec((1, SSM_WIDTH)), _const_spec((SSM_WIDTH, SSM_WIDTH))],
        out_specs=[row_spec, _const_spec(st_shape), _const_spec(st_shape)],
        out_shape=[jax.ShapeDtypeStruct((rows, SSM_WIDTH), BF16),
                   jax.ShapeDtypeStruct(st_shape, F32), jax.ShapeDtypeStruct(st_shape, F32)],
        scratch_shapes=[pltpu.VMEM((blk, SSM_LANES), F32), pltpu.VMEM((blk, SSM_LANES), F32),
                        pltpu.VMEM(st_shape, F32), pltpu.VMEM(st_shape, F32)],
        compiler_params=pltpu.CompilerParams(
            dimension_semantics=("arbitrary",), vmem_limit_bytes=VMEM_LIMIT),
        name="s5_scan",
    )(u_tm, s0_re, s0_im, prm["lb_re"], prm["lb_im"], prm["bd_re"], prm["bd_im"],
      prm["cd_re"], prm["cd_im_neg"], prm["d"], prm["w_glu"])


def _ssm_params(a_re, a_im, log_dt, b_re, b_im, c_re, c_im, d, w_glu):
    dt = jnp.exp(log_dt)[:, None]
    mag = jnp.exp(a_re * dt)
    lb_re = mag * jnp.cos(a_im * dt)
    lb_im = mag * jnp.sin(a_im * dt)
    den = a_re * a_re + a_im * a_im
    nr, ni = lb_re - 1.0, lb_im
    f_re = (nr * a_re + ni * a_im) / den
    f_im = (ni * a_re - nr * a_im) / den
    bb_re = f_re[:, :, None] * b_re - f_im[:, :, None] * b_im
    bb_im = f_re[:, :, None] * b_im + f_im[:, :, None] * b_re
    gps = SSM_GROUPS // SSM_SLABS
    eye = jnp.eye(gps, dtype=F32)

    def b_slabs(bb):
        t = bb.transpose(0, 2, 1).reshape(SSM_SLABS, gps, SSM_GROUP, SSM_STATE)
        return jnp.einsum("mgcp,gh->mgchp", t, eye).reshape(SSM_SLABS, LANES, SLAB_STATE).astype(BF16)

    def c_slabs(c):
        t = c.transpose(0, 2, 1).reshape(SSM_SLABS, gps, SSM_STATE, SSM_GROUP)
        return jnp.einsum("mgpc,gh->mgphc", t, eye).reshape(SSM_SLABS, SLAB_STATE, LANES).astype(BF16)

    return dict(lb_re=lb_re.reshape(1, SSM_LANES), lb_im=lb_im.reshape(1, SSM_LANES),
                bd_re=b_slabs(bb_re), bd_im=b_slabs(bb_im),
                cd_re=c_slabs(c_re), cd_im_neg=c_slabs(-c_im),
                d=d.reshape(1, SSM_WIDTH), w_glu=w_glu.astype(BF16))


def _post_kernel(x_ref, oa_ref, os_ref, ga_ref, gs_ref, p_ref,
                 wba_ref, wbs_ref, wout_ref, wg_ref, wu_ref, wd_ref, wpg_ref, wpp_ref,
                 n_mix_post, n_ffn_pre, n_ffn_post, n_ple_pre, n_ple_post,
                 y_ref, *, nb, tm, ff_chunk):
    rows = nb * tm
    x = x_ref[...].reshape(rows, D_MODEL)
    oa = oa_ref[...].reshape(rows, SB_WIDTH)
    if nb == 1:
        os_ = os_ref[...]
    else:
        os_ = jnp.concatenate([os_ref[:, b * SSM_WIDTH:(b + 1) * SSM_WIDTH] for b in range(nb)], axis=0)
    merged = (_sigmoid(ga_ref[...].reshape(rows, D_MODEL)) * _dot(oa, wba_ref[...])
              + _sigmoid(gs_ref[...].reshape(rows, D_MODEL)) * _dot(os_, wbs_ref[...]))
    x = x + _rms(_dot(merged.astype(BF16), wout_ref[...]), n_mix_post[...])

    f = _rms(x, n_ffn_pre[...]).astype(BF16)
    ff = jnp.zeros((rows, D_MODEL), F32)
    for c in range(D_FF // ff_chunk):
        cols = slice(c * ff_chunk, (c + 1) * ff_chunk)
        g = _dot(f, wg_ref[:, cols])
        a = (g * _sigmoid(g)) * _dot(f, wu_ref[:, cols])
        ff = ff + _dot(a.astype(BF16), wd_ref[cols, :])
    x = x + _rms(ff, n_ffn_post[...])

    gate = _sigmoid(_dot(_rms(x, n_ple_pre[...]).astype(BF16), wpg_ref[...]))
    pe = gate * _dot(p_ref[...].reshape(rows, PLE_DIM).astype(BF16), wpp_ref[...])
    y_ref[...] = (x + _rms(pe, n_ple_post[...])).reshape(nb, tm, D_MODEL)


def _post(x, o_attn, o_ssm_tm, g_attn, g_ssm, p, W, *, nb, tm, ff_chunk=256):
    bsz, L, _ = x.shape
    grid = (bsz // nb, L // tm)
    tok = lambda width: pl.BlockSpec((nb, tm, width), lambda b, i: (b, i, 0))
    vec = _const_spec((1, D_MODEL))
    return pl.pallas_call(
        functools.partial(_post_kernel, nb=nb, tm=tm, ff_chunk=ff_chunk),
        grid=grid,
        in_specs=[tok(D_MODEL), tok(SB_WIDTH),
                  pl.BlockSpec((tm, nb * SSM_WIDTH), lambda b, i: (i, b)),
                  tok(D_MODEL), tok(D_MODEL), tok(PLE_DIM),
                  _const_spec((SB_WIDTH, D_MODEL)), _const_spec((SSM_WIDTH, D_MODEL)),
                  _const_spec((D_MODEL, D_MODEL)),
                  _const_spec((D_MODEL, D_FF)), _const_spec((D_MODEL, D_FF)), _const_spec((D_FF, D_MODEL)),
                  _const_spec((D_MODEL, D_MODEL)), _const_spec((PLE_DIM, D_MODEL)),
                  vec, vec, vec, vec, vec],
        out_specs=tok(D_MODEL),
        out_shape=jax.ShapeDtypeStruct((bsz, L, D_MODEL), F32),
        compiler_params=pltpu.CompilerParams(
            dimension_semantics=("parallel", "parallel"), vmem_limit_bytes=VMEM_LIMIT),
        name="post",
    )(x, o_attn, o_ssm_tm, g_attn, g_ssm, p,
      W["w_branch_attn"], W["w_branch_ssm"], W["w_out"], W["w_ffn_gate"], W["w_ffn_up"], W["w_ffn_down"],
      W["w_ple_gate"], W["w_ple_proj"],
      W["norm_mix_post"], W["norm_ffn_pre"], W["norm_ffn_post"], W["norm_ple_pre"], W["norm_ple_post"])


def _layer(x, p, cache, s_re0, s_im0, W, *, nb, tm, bq, tc, post_nb, post_tm):
    bsz, L, _ = x.shape
    q, k, v, u_tm, g_attn, g_ssm = _inproj(x, W["norm_mix_pre"], W["w_in"], W["w_kv_t"], nb=nb, tm=tm,
                                           kv_t=cache is None)
    if cache is None:
        o_attn = _attention(q, k, v, bq=bq)
        k, v = (jnp.transpose(a, (0, 3, 1, 2)) for a in (k, v))
    else:
        o_attn = _attention_cached(q, k, v, *cache)
    o_ssm_tm, s_re, s_im = _ssm(u_tm.reshape(L * bsz, SSM_WIDTH), s_re0.reshape(bsz, SSM_LANES),
                                s_im0.reshape(bsz, SSM_LANES), W["ssm"], bsz=bsz, tc=tc)
    y = _post(x, o_attn, o_ssm_tm.reshape(L, bsz * SSM_WIDTH), g_attn, g_ssm, p, W, nb=post_nb, tm=post_tm)
    heads = (bsz, L, SB_HEADS, SB_HEAD_DIM)
    state = (bsz, SSM_GROUPS, SSM_STATE)
    return y, k.reshape(heads), v.reshape(heads), s_re.reshape(state), s_im.reshape(state)


def kernel(x_prompt, x_sample, cache_k, cache_v, state_ssm_re, state_ssm_im, p_prompt, p_sample, norm_mix_pre, norm_mix_post, w_in, ssm_a_re, ssm_a_im, ssm_log_dt, ssm_b_re, ssm_b_im, ssm_c_re, ssm_c_im, ssm_d, w_glu, w_branch_attn, w_branch_ssm, w_out, norm_ffn_pre, norm_ffn_post, w_ffn_gate, w_ffn_up, w_ffn_down, norm_ple_pre, norm_ple_post, w_ple_gate, w_ple_proj):
    depth = w_in.shape[0]
    yp, ys = x_prompt, x_sample
    outs = [[] for _ in range(8)]
    cache_kt, cache_vt = (jnp.transpose(c, (0, 1, 3, 4, 2)) for c in (cache_k, cache_v))
    for i in range(depth):
        W = dict(
            norm_mix_pre=norm_mix_pre[i][None], norm_mix_post=norm_mix_post[i][None],
            norm_ffn_pre=norm_ffn_pre[i][None], norm_ffn_post=norm_ffn_post[i][None],
            norm_ple_pre=norm_ple_pre[i][None], norm_ple_post=norm_ple_post[i][None],
            w_in=w_in[i].astype(BF16), w_kv_t=w_in[i][:, SB_WIDTH:3 * SB_WIDTH].T.astype(BF16),
            w_branch_attn=w_branch_attn[i].astype(BF16),
            w_branch_ssm=w_branch_ssm[i].astype(BF16), w_out=w_out[i].astype(BF16),
            w_ffn_gate=w_ffn_gate[i].astype(BF16), w_ffn_up=w_ffn_up[i].astype(BF16),
            w_ffn_down=w_ffn_down[i].astype(BF16), w_ple_gate=w_ple_gate[i].astype(BF16),
            w_ple_proj=w_ple_proj[i].astype(BF16),
            ssm=_ssm_params(ssm_a_re[i], ssm_a_im[i], ssm_log_dt[i], ssm_b_re[i], ssm_b_im[i],
                            ssm_c_re[i], ssm_c_im[i], ssm_d[i], w_glu[i]))
        bp, lp = yp.shape[0], yp.shape[1]
        bs, ls = ys.shape[0], ys.shape[1]
        zero_state = jnp.zeros((bp, SSM_GROUPS, SSM_STATE), F32)
        yp, kp, vp, srp, sip = _layer(
            yp, p_prompt[i], None, zero_state, zero_state, W,
            nb=1, tm=min(512, lp), bq=min(256, lp), tc=512 // bp, post_nb=1, post_tm=min(256, lp))
        ys, kn, vn, srs, sis = _layer(
            ys, p_sample[i], (cache_kt, cache_vt, i), state_ssm_re[i], state_ssm_im[i], W,
            nb=512 // ls, tm=ls, bq=ls, tc=512 // bs, post_nb=256 // ls, post_tm=ls)
        for lst, val in zip(outs, (kp, vp, srp, sip, kn, vn, srs, sis)):
            lst.append(val)
    return (yp, ys) + tuple(jnp.stack(o) for o in outs)
```

```python
import functools
import math

import jax
import jax.numpy as jnp
from jax import lax
from jax.experimental import pallas as pl
from jax.experimental.pallas import tpu as pltpu

F32 = jnp.float32
BF16 = jnp.bfloat16

D_MODEL = 1024
PLE_DIM = 256
SB_WIDTH = 512
SB_HEAD_DIM = 64
SB_HEADS = 8
SSM_WIDTH = 512
SSM_GROUP = 16
SSM_GROUPS = 32
SSM_STATE = 64
SSM_LANES = SSM_GROUPS * SSM_STATE
D_FF = 2816
IN_WIDTH = 3 * SB_WIDTH + SSM_WIDTH + 2 * D_MODEL
RMS_EPS = 1e-6

LANES = 128
HEADS_PER_BLOCK = LANES // SB_HEAD_DIM
SSM_SLABS = SSM_WIDTH // LANES
SLAB_STATE = SSM_LANES // SSM_SLABS
VMEM_LIMIT = 56 * 1024 * 1024


def _const_spec(shape):
    nd = len(shape)
    return pl.BlockSpec(shape, lambda *_: (0,) * nd, pipeline_mode=pl.Buffered(1))


def _rms(x, gain):
    ms = jnp.mean(x * x, axis=-1, keepdims=True)
    return x * lax.rsqrt(ms + RMS_EPS) * gain


def _sigmoid(x):
    return 1.0 / (1.0 + jnp.exp(-x))


def _dot(a, b):
    return jnp.dot(a, b, preferred_element_type=F32)


def _dot_nt(a, b):
    return lax.dot_general(a, b, (((1,), (1,)), ((), ())), preferred_element_type=F32)


def _inproj_kernel(*refs, nb, tm, kv_t):
    if kv_t:
        x_ref, gain_ref, w_ref, wkvt_ref, q_ref, k_ref, v_ref, u_ref, ga_ref, gs_ref = refs
    else:
        x_ref, gain_ref, w_ref, q_ref, k_ref, v_ref, u_ref, ga_ref, gs_ref = refs
    x = x_ref[...].reshape(nb * tm, D_MODEL)
    h = _rms(x, gain_ref[...]).astype(BF16)

    def proj(lo, width):
        return _dot(h, w_ref[:, lo:lo + width])

    q_ref[...] = (proj(0, SB_WIDTH) * (SB_HEAD_DIM ** -0.5)).astype(BF16).reshape(nb, tm, SB_WIDTH)
    if kv_t:
        k_ref[...] = _dot_nt(wkvt_ref[0:SB_WIDTH, :], h).reshape(SB_HEADS, SB_HEAD_DIM, tm)
        v_ref[...] = _dot_nt(wkvt_ref[SB_WIDTH:2 * SB_WIDTH, :], h).reshape(SB_HEADS, SB_HEAD_DIM, tm)
    else:
        k_ref[...] = proj(SB_WIDTH, SB_WIDTH).reshape(nb, tm, SB_WIDTH)
        v_ref[...] = proj(2 * SB_WIDTH, SB_WIDTH).reshape(nb, tm, SB_WIDTH)
    u = proj(3 * SB_WIDTH, SSM_WIDTH)
    for b in range(nb):
        u_ref[:, b * SSM_WIDTH:(b + 1) * SSM_WIDTH] = u[b * tm:(b + 1) * tm, :]
    off = 3 * SB_WIDTH + SSM_WIDTH
    ga_ref[...] = proj(off, D_MODEL).reshape(nb, tm, D_MODEL)
    gs_ref[...] = proj(off + D_MODEL, D_MODEL).reshape(nb, tm, D_MODEL)


def _inproj(x, gain, w_in, w_kv_t, *, nb, tm, kv_t):
    bsz, L, _ = x.shape
    grid = (bsz // nb, L // tm)
    tok = lambda width: pl.BlockSpec((nb, tm, width), lambda b, i: (b, i, 0))
    in_specs = [tok(D_MODEL), _const_spec((1, D_MODEL)), _const_spec((D_MODEL, IN_WIDTH))]
    args = [x, gain, w_in]
    if kv_t:
        assert nb == 1
        in_specs.append(_const_spec((2 * SB_WIDTH, D_MODEL)))
        args.append(w_kv_t)
        kv_spec = pl.BlockSpec((None, SB_HEADS, SB_HEAD_DIM, tm), lambda b, i: (b, 0, 0, i))
        kv_shape = jax.ShapeDtypeStruct((bsz, SB_HEADS, SB_HEAD_DIM, L), F32)
    else:
        kv_spec = tok(SB_WIDTH)
        kv_shape = jax.ShapeDtypeStruct((bsz, L, SB_WIDTH), F32)
    return pl.pallas_call(
        functools.partial(_inproj_kernel, nb=nb, tm=tm, kv_t=kv_t),
        grid=grid,
        in_specs=in_specs,
        out_specs=[tok(SB_WIDTH), kv_spec, kv_spec,
                   pl.BlockSpec((tm, nb * SSM_WIDTH), lambda b, i: (i, b)),
                   tok(D_MODEL), tok(D_MODEL)],
        out_shape=[jax.ShapeDtypeStruct((bsz, L, SB_WIDTH), BF16), kv_shape, kv_shape,
                   jax.ShapeDtypeStruct((L, bsz * SSM_WIDTH), F32),
                   jax.ShapeDtypeStruct((bsz, L, D_MODEL), F32),
                   jax.ShapeDtypeStruct((bsz, L, D_MODEL), F32)],
        compiler_params=pltpu.CompilerParams(
            dimension_semantics=("parallel", "parallel"), vmem_limit_bytes=VMEM_LIMIT),
        name="inproj",
    )(*args)


LOG2E = 1.0 / math.log(2.0)
SKIP_LOG2 = 105.0 * LOG2E


def _softplus2(y):
    return jnp.maximum(y, 0.0) + jnp.log2(1.0 + jnp.exp2(-jnp.abs(y)))


def _suffix_sums(sp, tri):
    hi = sp.astype(BF16)
    lo = (sp - hi.astype(F32)).astype(BF16)
    return _dot(hi, tri) + _dot(lo, tri)


def _sb_weights(y, cum, c, visible):
    w = jnp.exp2(y - cum - c)
    if visible is not None:
        w = jnp.where(visible, w, 0.0)
    return w.astype(BF16)


def _unfinished(*cs):
    m = jnp.min(cs[0])
    for c in cs[1:]:
        m = jnp.minimum(m, jnp.min(c))
    return (m < SKIP_LOG2).astype(jnp.int32)


def _more(st):
    return (st[0] >= 0) & (st[1] > 0)


def _attn_kernel(q_ref, k_ref, v_ref, o_ref, c_ref, acc_ref, *, bq):
    i = pl.program_id(1)
    n_pairs = SB_WIDTH // LANES
    lane = lax.broadcasted_iota(jnp.int32, (bq, LANES), 1)
    row = lax.broadcasted_iota(jnp.int32, (bq, bq), 0)
    col = lax.broadcasted_iota(jnp.int32, (bq, bq), 1)
    tri = (row >= col).astype(BF16)
    diag_visible = col < row

    in_head = [(lane >= h * SB_HEAD_DIM) & (lane < (h + 1) * SB_HEAD_DIM) for h in range(HEADS_PER_BLOCK)]
    qs = []
    for p in range(n_pairs):
        qp = q_ref[:, p * LANES:(p + 1) * LANES]
        qs += [jnp.where(m, qp, jnp.zeros_like(qp)) for m in in_head]

    def visit(j, visible, first):
        s0 = pl.multiple_of(j * bq, bq)
        kts, vts = [], []
        for p in range(n_pairs):
            heads = slice(p * HEADS_PER_BLOCK, (p + 1) * HEADS_PER_BLOCK)
            kts.append(k_ref[heads, :, pl.ds(s0, bq)].reshape(LANES, bq).astype(BF16))
            vts.append(v_ref[heads, :, pl.ds(s0, bq)].reshape(LANES, bq).astype(BF16))
        ys = [_dot(qs[h], kts[h // HEADS_PER_BLOCK]) * LOG2E for h in range(SB_HEADS)]
        sps = [_softplus2(y) if visible is None else jnp.where(visible, _softplus2(y), 0.0) for y in ys]
        cums = [_suffix_sums(sp, tri) for sp in sps]
        cs_old = [0.0 if first else c_ref[h] for h in range(SB_HEADS)]
        ws = [_sb_weights(ys[h], cums[h], cs_old[h], visible) for h in range(SB_HEADS)]
        accs = [_dot_nt(ws[h], vts[h // HEADS_PER_BLOCK]) for h in range(SB_HEADS)]
        cs = [cums[h][:, 0:1] + cs_old[h] for h in range(SB_HEADS)]
        for h in range(SB_HEADS):
            acc_ref[h] = accs[h] if first else acc_ref[h] + accs[h]
            c_ref[h] = cs[h]
        return _unfinished(*cs)

    go = visit(i, diag_visible, True)
    lax.while_loop(_more, lambda st: (st[0] - 1, visit(st[0], None, False)), (i - 1, go))
    for p in range(n_pairs):
        o_ref[:, p * LANES:(p + 1) * LANES] = jnp.where(
            in_head[0], acc_ref[p * HEADS_PER_BLOCK], acc_ref[p * HEADS_PER_BLOCK + 1]).astype(o_ref.dtype)


def _attention(q, k_t, v_t, *, bq):
    bsz, L, _ = q.shape
    qspec = pl.BlockSpec((None, bq, SB_WIDTH), lambda b, i: (b, i, 0))
    kvspec = pl.BlockSpec((None, SB_HEADS, SB_HEAD_DIM, L), lambda b, i: (b, 0, 0, 0))
    return pl.pallas_call(
        functools.partial(_attn_kernel, bq=bq),
        grid=(bsz, L // bq),
        in_specs=[qspec, kvspec, kvspec],
        out_specs=qspec,
        out_shape=jax.ShapeDtypeStruct((bsz, L, SB_WIDTH), BF16),
        scratch_shapes=[pltpu.VMEM((SB_HEADS, bq, 1), F32), pltpu.VMEM((SB_HEADS, bq, LANES), F32)],
        compiler_params=pltpu.CompilerParams(
            dimension_semantics=("parallel", "parallel"), vmem_limit_bytes=VMEM_LIMIT),
        name="sb_attention",
    )(q, k_t, v_t)


def _attn_cached_kernel(q_ref, k_ref, v_ref, kl_ref, vl_ref, kc_hbm, vc_hbm, o_ref, kbuf, vbuf, sem,
                        *, layer, lq, bk, n_past):
    b = pl.program_id(0)
    hd = SB_HEAD_DIM
    q = q_ref[...]
    kn = k_ref[...].astype(BF16)
    vn = v_ref[...].astype(BF16)
    head = lambda x, h: x[:, h * hd:(h + 1) * hd]
    qs = [head(q, h) for h in range(SB_HEADS)]
    rows = SB_HEADS * lq

    r = lax.broadcasted_iota(jnp.int32, (rows, lq), 0) % lq
    s = lax.broadcasted_iota(jnp.int32, (rows, lq), 1)
    visible = s < r
    tr = lax.broadcasted_iota(jnp.int32, (lq, lq), 0)
    tc = lax.broadcasted_iota(jnp.int32, (lq, lq), 1)
    yd = jnp.concatenate([_dot_nt(qs[h], head(kn, h)) for h in range(SB_HEADS)], axis=0) * LOG2E
    cumd = _suffix_sums(jnp.where(visible, _softplus2(yd), 0.0), (tr >= tc).astype(BF16))
    wd = _sb_weights(yd, cumd, 0.0, visible)
    accs = [_dot(wd[h * lq:(h + 1) * lq], head(vn, h)) for h in range(SB_HEADS)]
    c = cumd[:, 0:1]

    pr = lax.broadcasted_iota(jnp.int32, (bk, bk), 0)
    pc = lax.broadcasted_iota(jnp.int32, (bk, bk), 1)
    tri_p = (pr >= pc).astype(BF16)

    def cache_block(kref, vref, c, accs):
        y = jnp.concatenate([_dot(qs[h], kref[h].astype(BF16)) for h in range(SB_HEADS)], axis=0) * LOG2E
        cum = _suffix_sums(_softplus2(y), tri_p)
        w = _sb_weights(y, cum, c, None)
        accs = [accs[h] + _dot_nt(w[h * lq:(h + 1) * lq], vref[h].astype(BF16)) for h in range(SB_HEADS)]
        return c + cum[:, 0:1], accs

    c, accs = cache_block(kl_ref, vl_ref, c, accs)

    def fetch(j):
        s0 = pl.multiple_of(j * bk, bk)
        return (pltpu.make_async_copy(kc_hbm.at[layer, b, :, :, pl.ds(s0, bk)], kbuf, sem.at[0]),
                pltpu.make_async_copy(vc_hbm.at[layer, b, :, :, pl.ds(s0, bk)], vbuf, sem.at[1]))

    def visit(st):
        j, c, accs = st[0], st[2], list(st[3:])
        copies = fetch(j)
        for cp in copies:
            cp.start()
        for cp in copies:
            cp.wait()
        c, accs = cache_block(kbuf, vbuf, c, accs)
        return (j - 1, _unfinished(c), c) + tuple(accs)

    st = lax.while_loop(_more, visit, (n_past - 2, _unfinished(c), c) + tuple(accs))
    o_ref[...] = jnp.concatenate(st[3:], axis=1).astype(o_ref.dtype)


def _attention_cached(q, k, v, cache_kt, cache_vt, layer, *, bk=256):
    bsz, lq, _ = q.shape
    past = cache_kt.shape[4]
    n_past = past // bk
    tok = pl.BlockSpec((None, lq, SB_WIDTH), lambda b: (b, 0, 0))
    last = pl.BlockSpec((None, None, SB_HEADS, SB_HEAD_DIM, bk), lambda b: (layer, b, 0, 0, n_past - 1))
    hbm = pl.BlockSpec(memory_space=pl.ANY)
    return pl.pallas_call(
        functools.partial(_attn_cached_kernel, layer=layer, lq=lq, bk=bk, n_past=n_past),
        grid=(bsz,),
        in_specs=[tok, tok, tok, last, last, hbm, hbm],
        out_specs=tok,
        out_shape=jax.ShapeDtypeStruct((bsz, lq, SB_WIDTH), BF16),
        scratch_shapes=[pltpu.VMEM((SB_HEADS, SB_HEAD_DIM, bk), F32), pltpu.VMEM((SB_HEADS, SB_HEAD_DIM, bk), F32),
                        pltpu.SemaphoreType.DMA((2,))],
        compiler_params=pltpu.CompilerParams(
            dimension_semantics=("parallel",), vmem_limit_bytes=VMEM_LIMIT),
        name="sb_attention_cached",
    )(q, k, v, cache_kt, cache_vt, cache_kt, cache_vt)


def _gelu_tanh(x):
    return 0.5 * x * (1.0 + jnp.tanh(math.sqrt(2.0 / math.pi) * (x + 0.044715 * (x * x * x))))


def _ssm_kernel(u_ref, s0re_ref, s0im_ref, lbre_ref, lbim_ref, bre_ref, bim_ref, cre_ref, cimn_ref,
                d_ref, wglu_ref, o_ref, sfre_ref, sfim_ref, sre, sim, st_re, st_im, *, bsz, tc, lw):
    step = pl.program_id(0)

    @pl.when(step == 0)
    def _():
        st_re[...] = s0re_ref[...]
        st_im[...] = s0im_ref[...]

    u = u_ref[...]
    ub = u.astype(BF16)
    for m in range(SSM_SLABS):
        um = ub[:, m * LANES:(m + 1) * LANES]
        sre[:, m * SLAB_STATE:(m + 1) * SLAB_STATE] = _dot(um, bre_ref[m])
        sim[:, m * SLAB_STATE:(m + 1) * SLAB_STATE] = _dot(um, bim_ref[m])

    for lc in range(SSM_LANES // lw):
        lanes = slice(lc * lw, (lc + 1) * lw)
        lr = jnp.broadcast_to(lbre_ref[:, lanes], (8, lw))
        li = jnp.broadcast_to(lbim_ref[:, lanes], (8, lw))
        for sb in range(bsz // 8):
            subl = slice(sb * 8, (sb + 1) * 8)

            def body(t, carry):
                sr, si = carry
                r0 = pl.multiple_of(t * bsz + sb * 8, 8)
                nr = lr * sr - li * si + sre[pl.ds(r0, 8), lanes]
                ni = lr * si + li * sr + sim[pl.ds(r0, 8), lanes]
                sre[pl.ds(r0, 8), lanes] = nr
                sim[pl.ds(r0, 8), lanes] = ni
                return nr, ni

            sr, si = lax.fori_loop(0, tc, body, (st_re[subl, lanes], st_im[subl, lanes]), unroll=4)
            st_re[subl, lanes] = sr
            st_im[subl, lanes] = si

    ys = []
    for m in range(SSM_SLABS):
        slab = slice(m * SLAB_STATE, (m + 1) * SLAB_STATE)
        ys.append(_dot(sre[:, slab].astype(BF16), cre_ref[m]) + _dot(sim[:, slab].astype(BF16), cimn_ref[m]))
    y = jnp.concatenate(ys, axis=1) + d_ref[...] * u
    y = _gelu_tanh(y)
    o_ref[...] = (y * _sigmoid(_dot(y.astype(BF16), wglu_ref[...]))).astype(o_ref.dtype)

    @pl.when(step == pl.num_programs(0) - 1)
    def _():
        sfre_ref[...] = st_re[...]
        sfim_ref[...] = st_im[...]


def _ssm(u_tm, s0_re, s0_im, prm, *, bsz, tc, lw=512):
    rows = u_tm.shape[0]
    blk = tc * bsz
    row_spec = pl.BlockSpec((blk, SSM_WIDTH), lambda s: (s, 0))
    st_shape = (bsz, SSM_LANES)
    return pl.pallas_call(
        functools.partial(_ssm_kernel, bsz=bsz, tc=tc, lw=lw),
        grid=(rows // blk,),
        in_specs=[row_spec, _const_spec(st_shape), _const_spec(st_shape),
                  _const_spec((1, SSM_LANES)), _const_spec((1, SSM_LANES)),
                  _const_spec((SSM_SLABS, LANES, SLAB_STATE)), _const_spec((SSM_SLABS, LANES, SLAB_STATE)),
                  _const_spec((SSM_SLABS, SLAB_STATE, LANES)), _const_spec((SSM_SLABS, SLAB_STATE, LANES)),
                  _const_spec((1, SSM_WIDTH)), _const_spec((SSM_WIDTH, SSM_WIDTH))],
        out_specs=[row_spec, _const_spec(st_shape), _const_spec(st_shape)],
        out_shape=[jax.ShapeDtypeStruct((rows, SSM_WIDTH), BF16),
                   jax.ShapeDtypeStruct(st_shape, F32), jax.ShapeDtypeStruct(st_shape, F32)],
        scratch_shapes=[pltpu.VMEM((blk, SSM_LANES), F32), pltpu.VMEM((blk, SSM_LANES), F32),
                        pltpu.VMEM(st_shape, F32), pltpu.VMEM(st_shape, F32)],
        compiler_params=pltpu.CompilerParams(
            dimension_semantics=("arbitrary",), vmem_limit_bytes=VMEM_LIMIT),
        name="s5_scan",
    )(u_tm, s0_re, s0_im, prm["lb_re"], prm["lb_im"], prm["bd_re"], prm["bd_im"],
      prm["cd_re"], prm["cd_im_neg"], prm["d"], prm["w_glu"])


def _ssm_params(a_re, a_im, log_dt, b_re, b_im, c_re, c_im, d, w_glu):
    dt = jnp.exp(log_dt)[:, None]
    mag = jnp.exp(a_re * dt)
    lb_re = mag * jnp.cos(a_im * dt)
    lb_im = mag * jnp.sin(a_im * dt)
    den = a_re * a_re + a_im * a_im
    nr, ni = lb_re - 1.0, lb_im
    f_re = (nr * a_re + ni * a_im) / den
    f_im = (ni * a_re - nr * a_im) / den
    bb_re = f_re[:, :, None] * b_re - f_im[:, :, None] * b_im
    bb_im = f_re[:, :, None] * b_im + f_im[:, :, None] * b_re
    gps = SSM_GROUPS // SSM_SLABS
    eye = jnp.eye(gps, dtype=F32)

    def b_slabs(bb):
        t = bb.transpose(0, 2, 1).reshape(SSM_SLABS, gps, SSM_GROUP, SSM_STATE)
        return jnp.einsum("mgcp,gh->mgchp", t, eye).reshape(SSM_SLABS, LANES, SLAB_STATE).astype(BF16)

    def c_slabs(c):
        t = c.transpose(0, 2, 1).reshape(SSM_SLABS, gps, SSM_STATE, SSM_GROUP)
        return jnp.einsum("mgpc,gh->mgphc", t, eye).reshape(SSM_SLABS, SLAB_STATE, LANES).astype(BF16)

    return dict(lb_re=lb_re.reshape(1, SSM_LANES), lb_im=lb_im.reshape(1, SSM_LANES),
                bd_re=b_slabs(bb_re), bd_im=b_slabs(bb_im),
                cd_re=c_slabs(c_re), cd_im_neg=c_slabs(-c_im),
                d=d.reshape(1, SSM_WIDTH), w_glu=w_glu.astype(BF16))


def _post_kernel(x_ref, oa_ref, os_ref, ga_ref, gs_ref, p_ref,
                 wba_ref, wbs_ref, wout_ref, wg_ref, wu_ref, wd_ref, wpg_ref, wpp_ref,
                 n_mix_post, n_ffn_pre, n_ffn_post, n_ple_pre, n_ple_post,
                 y_ref, *, nb, tm, ff_chunk):
    rows = nb * tm
    x = x_ref[...].reshape(rows, D_MODEL)
    oa = oa_ref[...].reshape(rows, SB_WIDTH)
    if nb == 1:
        os_ = os_ref[...]
    else:
        os_ = jnp.concatenate([os_ref[:, b * SSM_WIDTH:(b + 1) * SSM_WIDTH] for b in range(nb)], axis=0)
    merged = (_sigmoid(ga_ref[...].reshape(rows, D_MODEL)) * _dot(oa, wba_ref[...])
              + _sigmoid(gs_ref[...].reshape(rows, D_MODEL)) * _dot(os_, wbs_ref[...]))
    x = x + _rms(_dot(merged.astype(BF16), wout_ref[...]), n_mix_post[...])

    f = _rms(x, n_ffn_pre[...]).astype(BF16)
    ff = jnp.zeros((rows, D_MODEL), F32)
    for c in range(D_FF // ff_chunk):
        cols = slice(c * ff_chunk, (c + 1) * ff_chunk)
        g = _dot(f, wg_ref[:, cols])
        a = (g * _sigmoid(g)) * _dot(f, wu_ref[:, cols])
        ff = ff + _dot(a.astype(BF16), wd_ref[cols, :])
    x = x + _rms(ff, n_ffn_post[...])

    gate = _sigmoid(_dot(_rms(x, n_ple_pre[...]).astype(BF16), wpg_ref[...]))
    pe = gate * _dot(p_ref[...].reshape(rows, PLE_DIM).astype(BF16), wpp_ref[...])
    y_ref[...] = (x + _rms(pe, n_ple_post[...])).reshape(nb, tm, D_MODEL)


def _post(x, o_attn, o_ssm_tm, g_attn, g_ssm, p, W, *, nb, tm, ff_chunk=256):
    bsz, L, _ = x.shape
    grid = (bsz // nb, L // tm)
    tok = lambda width: pl.BlockSpec((nb, tm, width), lambda b, i: (b, i, 0))
    vec = _const_spec((1, D_MODEL))
    return pl.pallas_call(
        functools.partial(_post_kernel, nb=nb, tm=tm, ff_chunk=ff_chunk),
        grid=grid,
        in_specs=[tok(D_MODEL), tok(SB_WIDTH),
                  pl.BlockSpec((tm, nb * SSM_WIDTH), lambda b, i: (i, b)),
                  tok(D_MODEL), tok(D_MODEL), tok(PLE_DIM),
                  _const_spec((SB_WIDTH, D_MODEL)), _const_spec((SSM_WIDTH, D_MODEL)),
                  _const_spec((D_MODEL, D_MODEL)),
                  _const_spec((D_MODEL, D_FF)), _const_spec((D_MODEL, D_FF)), _const_spec((D_FF, D_MODEL)),
                  _const_spec((D_MODEL, D_MODEL)), _const_spec((PLE_DIM, D_MODEL)),
                  vec, vec, vec, vec, vec],
        out_specs=tok(D_MODEL),
        out_shape=jax.ShapeDtypeStruct((bsz, L, D_MODEL), F32),
        compiler_params=pltpu.CompilerParams(
            dimension_semantics=("parallel", "parallel"), vmem_limit_bytes=VMEM_LIMIT),
        name="post",
    )(x, o_attn, o_ssm_tm, g_attn, g_ssm, p,
      W["w_branch_attn"], W["w_branch_ssm"], W["w_out"], W["w_ffn_gate"], W["w_ffn_up"], W["w_ffn_down"],
      W["w_ple_gate"], W["w_ple_proj"],
      W["norm_mix_post"], W["norm_ffn_pre"], W["norm_ffn_post"], W["norm_ple_pre"], W["norm_ple_post"])


def _layer(x, p, cache, s_re0, s_im0, W, *, nb, tm, bq, tc, post_nb, post_tm):
    bsz, L, _ = x.shape
    q, k, v, u_tm, g_attn, g_ssm = _inproj(x, W["norm_mix_pre"], W["w_in"], W["w_kv_t"], nb=nb, tm=tm,
                                           kv_t=cache is None)
    if cache is None:
        o_attn = _attention(q, k, v, bq=bq)
        k, v = (jnp.transpose(a, (0, 3, 1, 2)) for a in (k, v))
    else:
        o_attn = _attention_cached(q, k, v, *cache)
    o_ssm_tm, s_re, s_im = _ssm(u_tm.reshape(L * bsz, SSM_WIDTH), s_re0.reshape(bsz, SSM_LANES),
                                s_im0.reshape(bsz, SSM_LANES), W["ssm"], bsz=bsz, tc=tc)
    y = _post(x, o_attn, o_ssm_tm.reshape(L, bsz * SSM_WIDTH), g_attn, g_ssm, p, W, nb=post_nb, tm=post_tm)
    heads = (bsz, L, SB_HEADS, SB_HEAD_DIM)
    state = (bsz, SSM_GROUPS, SSM_STATE)
    return y, k.reshape(heads), v.reshape(heads), s_re.reshape(state), s_im.reshape(state)


def kernel(x_prompt, x_sample, cache_k, cache_v, state_ssm_re, state_ssm_im, p_prompt, p_sample, norm_mix_pre, norm_mix_post, w_in, ssm_a_re, ssm_a_im, ssm_log_dt, ssm_b_re, ssm_b_im, ssm_c_re, ssm_c_im, ssm_d, w_glu, w_branch_attn, w_branch_ssm, w_out, norm_ffn_pre, norm_ffn_post, w_ffn_gate, w_ffn_up, w_ffn_down, norm_ple_pre, norm_ple_post, w_ple_gate, w_ple_proj):
    depth = w_in.shape[0]
    yp, ys = x_prompt, x_sample
    outs = [[] for _ in range(8)]
    cache_kt, cache_vt = (jnp.transpose(c, (0, 1, 3, 4, 2)) for c in (cache_k, cache_v))
    for i in range(depth):
        W = dict(
            norm_mix_pre=norm_mix_pre[i][None], norm_mix_post=norm_mix_post[i][None],
            norm_ffn_pre=norm_ffn_pre[i][None], norm_ffn_post=norm_ffn_post[i][None],
            norm_ple_pre=norm_ple_pre[i][None], norm_ple_post=norm_ple_post[i][None],
            w_in=w_in[i].astype(BF16), w_kv_t=w_in[i][:, SB_WIDTH:3 * SB_WIDTH].T.astype(BF16),
            w_branch_attn=w_branch_attn[i].astype(BF16),
            w_branch_ssm=w_branch_ssm[i].astype(BF16), w_out=w_out[i].astype(BF16),
            w_ffn_gate=w_ffn_gate[i].astype(BF16), w_ffn_up=w_ffn_up[i].astype(BF16),
            w_ffn_down=w_ffn_down[i].astype(BF16), w_ple_gate=w_ple_gate[i].astype(BF16),
            w_ple_proj=w_ple_proj[i].astype(BF16),
            ssm=_ssm_params(ssm_a_re[i], ssm_a_im[i], ssm_log_dt[i], ssm_b_re[i], ssm_b_im[i],
                            ssm_c_re[i], ssm_c_im[i], ssm_d[i], w_glu[i]))
        bp, lp = yp.shape[0], yp.shape[1]
        bs, ls = ys.shape[0], ys.shape[1]
        zero_state = jnp.zeros((bp, SSM_GROUPS, SSM_STATE), F32)
        yp, kp, vp, srp, sip = _layer(
            yp, p_prompt[i], None, zero_state, zero_state, W,
            nb=1, tm=min(512, lp), bq=min(256, lp), tc=512 // bp, post_nb=1, post_tm=min(256, lp))
        ys, kn, vn, srs, sis = _layer(
            ys, p_sample[i], (cache_kt, cache_vt, i), state_ssm_re[i], state_ssm_im[i], W,
            nb=512 // ls, tm=ls, bq=ls, tc=512 // bs, post_nb=256 // ls, post_tm=ls)
        for lst, val in zip(outs, (kp, vp, srp, sip, kn, vn, srs, sis)):
            lst.append(val)
    return (yp, ys) + tuple(jnp.stack(o) for o in outs)
```

```python
import functools
import math

import jax
import jax.numpy as jnp
from jax import lax
from jax.experimental import pallas as pl
from jax.experimental.pallas import tpu as pltpu

F32 = jnp.float32
BF16 = jnp.bfloat16

D_MODEL = 1024
PLE_DIM = 256
SB_WIDTH = 512
SB_HEAD_DIM = 64
SB_HEADS = 8
SSM_WIDTH = 512
SSM_GROUP = 16
SSM_GROUPS = 32
SSM_STATE = 64
SSM_LANES = SSM_GROUPS * SSM_STATE
D_FF = 2816
IN_WIDTH = 3 * SB_WIDTH + SSM_WIDTH + 2 * D_MODEL
RMS_EPS = 1e-6

LANES = 128
HEADS_PER_BLOCK = LANES // SB_HEAD_DIM
SSM_SLABS = SSM_WIDTH // LANES
SLAB_STATE = SSM_LANES // SSM_SLABS
VMEM_LIMIT = 56 * 1024 * 1024


def _const_spec(shape):
    nd = len(shape)
    return pl.BlockSpec(shape, lambda *_: (0,) * nd, pipeline_mode=pl.Buffered(1))


def _rms(x, gain):
    ms = jnp.mean(x * x, axis=-1, keepdims=True)
    return x * lax.rsqrt(ms + RMS_EPS) * gain


def _sigmoid(x):
    return 1.0 / (1.0 + jnp.exp(-x))


def _dot(a, b):
    return jnp.dot(a, b, preferred_element_type=F32)


def _dot_nt(a, b):
    return lax.dot_general(a, b, (((1,), (1,)), ((), ())), preferred_element_type=F32)


def _inproj_kernel(*refs, nb, tm, kv_t):
    if kv_t:
        x_ref, gain_ref, w_ref, wkvt_ref, q_ref, k_ref, v_ref, u_ref, ga_ref, gs_ref = refs
    else:
        x_ref, gain_ref, w_ref, q_ref, k_ref, v_ref, u_ref, ga_ref, gs_ref = refs
    x = x_ref[...].reshape(nb * tm, D_MODEL)
    h = _rms(x, gain_ref[...]).astype(BF16)

    def proj(lo, width):
        return _dot(h, w_ref[:, lo:lo + width])

    q_ref[...] = (proj(0, SB_WIDTH) * (SB_HEAD_DIM ** -0.5)).astype(BF16).reshape(nb, tm, SB_WIDTH)
    if kv_t:
        k_ref[...] = _dot_nt(wkvt_ref[0:SB_WIDTH, :], h).reshape(SB_HEADS, SB_HEAD_DIM, tm)
        v_ref[...] = _dot_nt(wkvt_ref[SB_WIDTH:2 * SB_WIDTH, :], h).reshape(SB_HEADS, SB_HEAD_DIM, tm)
    else:
        k_ref[...] = proj(SB_WIDTH, SB_WIDTH).reshape(nb, tm, SB_WIDTH)
        v_ref[...] = proj(2 * SB_WIDTH, SB_WIDTH).reshape(nb, tm, SB_WIDTH)
    u_ref[...] = proj(3 * SB_WIDTH, SSM_WIDTH).reshape(nb, tm, SSM_WIDTH)
    off = 3 * SB_WIDTH + SSM_WIDTH
    ga_ref[...] = proj(off, D_MODEL).reshape(nb, tm, D_MODEL)
    gs_ref[...] = proj(off + D_MODEL, D_MODEL).reshape(nb, tm, D_MODEL)


def _inproj(x, gain, w_in, w_kv_t, *, nb, tm, kv_t):
    bsz, L, _ = x.shape
    grid = (bsz // nb, L // tm)
    tok = lambda width: pl.BlockSpec((nb, tm, width), lambda b, i: (b, i, 0))
    in_specs = [tok(D_MODEL), _const_spec((1, D_MODEL)), _const_spec((D_MODEL, IN_WIDTH))]
    args = [x, gain, w_in]
    if kv_t:
        assert nb == 1
        in_specs.append(_const_spec((2 * SB_WIDTH, D_MODEL)))
        args.append(w_kv_t)
        kv_spec = pl.BlockSpec((None, SB_HEADS, SB_HEAD_DIM, tm), lambda b, i: (b, 0, 0, i))
        kv_shape = jax.ShapeDtypeStruct((bsz, SB_HEADS, SB_HEAD_DIM, L), F32)
    else:
        kv_spec = tok(SB_WIDTH)
        kv_shape = jax.ShapeDtypeStruct((bsz, L, SB_WIDTH), F32)
    return pl.pallas_call(
        functools.partial(_inproj_kernel, nb=nb, tm=tm, kv_t=kv_t),
        grid=grid,
        in_specs=in_specs,
        out_specs=[tok(SB_WIDTH), kv_spec, kv_spec, tok(SSM_WIDTH), tok(D_MODEL), tok(D_MODEL)],
        out_shape=[jax.ShapeDtypeStruct((bsz, L, SB_WIDTH), BF16), kv_shape, kv_shape,
                   jax.ShapeDtypeStruct((bsz, L, SSM_WIDTH), F32),
                   jax.ShapeDtypeStruct((bsz, L, D_MODEL), F32),
                   jax.ShapeDtypeStruct((bsz, L, D_MODEL), F32)],
        compiler_params=pltpu.CompilerParams(
            dimension_semantics=("parallel", "parallel"), vmem_limit_bytes=VMEM_LIMIT),
        name="inproj",
    )(*args)


LOG2E = 1.0 / math.log(2.0)
SKIP_LOG2 = 105.0 * LOG2E


def _softplus2(y):
    return jnp.maximum(y, 0.0) + jnp.log2(1.0 + jnp.exp2(-jnp.abs(y)))


def _suffix_sums(sp, tri):
    hi = sp.astype(BF16)
    lo = (sp - hi.astype(F32)).astype(BF16)
    return _dot(hi, tri) + _dot(lo, tri)


def _sb_weights(y, cum, c, visible):
    w = jnp.exp2(y - cum - c)
    if visible is not None:
        w = jnp.where(visible, w, 0.0)
    return w.astype(BF16)


def _unfinished(*cs):
    m = jnp.min(cs[0])
    for c in cs[1:]:
        m = jnp.minimum(m, jnp.min(c))
    return (m < SKIP_LOG2).astype(jnp.int32)


def _more(st):
    return (st[0] >= 0) & (st[1] > 0)


def _attn_kernel(q_ref, k_ref, v_ref, o_ref, c_ref, acc_ref, *, bq):
    i = pl.program_id(1)
    n_pairs = SB_WIDTH // LANES
    lane = lax.broadcasted_iota(jnp.int32, (bq, LANES), 1)
    row = lax.broadcasted_iota(jnp.int32, (bq, bq), 0)
    col = lax.broadcasted_iota(jnp.int32, (bq, bq), 1)
    tri = (row >= col).astype(BF16)
    diag_visible = col < row

    in_head = [(lane >= h * SB_HEAD_DIM) & (lane < (h + 1) * SB_HEAD_DIM) for h in range(HEADS_PER_BLOCK)]
    qs = []
    for p in range(n_pairs):
        qp = q_ref[:, p * LANES:(p + 1) * LANES]
        qs += [jnp.where(m, qp, jnp.zeros_like(qp)) for m in in_head]

    def visit(j, visible, first):
        s0 = pl.multiple_of(j * bq, bq)
        kts, vts = [], []
        for p in range(n_pairs):
            heads = slice(p * HEADS_PER_BLOCK, (p + 1) * HEADS_PER_BLOCK)
            kts.append(k_ref[heads, :, pl.ds(s0, bq)].reshape(LANES, bq).astype(BF16))
            vts.append(v_ref[heads, :, pl.ds(s0, bq)].reshape(LANES, bq).astype(BF16))
        ys = [_dot(qs[h], kts[h // HEADS_PER_BLOCK]) * LOG2E for h in range(SB_HEADS)]
        sps = [_softplus2(y) if visible is None else jnp.where(visible, _softplus2(y), 0.0) for y in ys]
        cums = [_suffix_sums(sp, tri) for sp in sps]
        cs_old = [0.0 if first else c_ref[h] for h in range(SB_HEADS)]
        ws = [_sb_weights(ys[h], cums[h], cs_old[h], visible) for h in range(SB_HEADS)]
        accs = [_dot_nt(ws[h], vts[h // HEADS_PER_BLOCK]) for h in range(SB_HEADS)]
        cs = [cums[h][:, 0:1] + cs_old[h] for h in range(SB_HEADS)]
        for h in range(SB_HEADS):
            acc_ref[h] = accs[h] if first else acc_ref[h] + accs[h]
            c_ref[h] = cs[h]
        return _unfinished(*cs)

    go = visit(i, diag_visible, True)
    lax.while_loop(_more, lambda st: (st[0] - 1, visit(st[0], None, False)), (i - 1, go))
    for p in range(n_pairs):
        o_ref[:, p * LANES:(p + 1) * LANES] = jnp.where(
            in_head[0], acc_ref[p * HEADS_PER_BLOCK], acc_ref[p * HEADS_PER_BLOCK + 1]).astype(o_ref.dtype)


def _attention(q, k_t, v_t, *, bq):
    bsz, L, _ = q.shape
    qspec = pl.BlockSpec((None, bq, SB_WIDTH), lambda b, i: (b, i, 0))
    kvspec = pl.BlockSpec((None, SB_HEADS, SB_HEAD_DIM, L), lambda b, i: (b, 0, 0, 0))
    return pl.pallas_call(
        functools.partial(_attn_kernel, bq=bq),
        grid=(bsz, L // bq),
        in_specs=[qspec, kvspec, kvspec],
        out_specs=qspec,
        out_shape=jax.ShapeDtypeStruct((bsz, L, SB_WIDTH), BF16),
        scratch_shapes=[pltpu.VMEM((SB_HEADS, bq, 1), F32), pltpu.VMEM((SB_HEADS, bq, LANES), F32)],
        compiler_params=pltpu.CompilerParams(
            dimension_semantics=("parallel", "parallel"), vmem_limit_bytes=VMEM_LIMIT),
        name="sb_attention",
    )(q, k_t, v_t)


def _attn_cached_kernel(q_ref, k_ref, v_ref, kl_ref, vl_ref, kc_hbm, vc_hbm, o_ref, kbuf, vbuf, sem,
                        *, layer, lq, bk, n_past):
    b = pl.program_id(0)
    hd = SB_HEAD_DIM
    q = q_ref[...]
    kn = k_ref[...].astype(BF16)
    vn = v_ref[...].astype(BF16)
    head = lambda x, h: x[:, h * hd:(h + 1) * hd]
    qs = [head(q, h) for h in range(SB_HEADS)]
    rows = SB_HEADS * lq

    r = lax.broadcasted_iota(jnp.int32, (rows, lq), 0) % lq
    s = lax.broadcasted_iota(jnp.int32, (rows, lq), 1)
    visible = s < r
    tr = lax.broadcasted_iota(jnp.int32, (lq, lq), 0)
    tc = lax.broadcasted_iota(jnp.int32, (lq, lq), 1)
    yd = jnp.concatenate([_dot_nt(qs[h], head(kn, h)) for h in range(SB_HEADS)], axis=0) * LOG2E
    cumd = _suffix_sums(jnp.where(visible, _softplus2(yd), 0.0), (tr >= tc).astype(BF16))
    wd = _sb_weights(yd, cumd, 0.0, visible)
    accs = [_dot(wd[h * lq:(h + 1) * lq], head(vn, h)) for h in range(SB_HEADS)]
    c = cumd[:, 0:1]

    pr = lax.broadcasted_iota(jnp.int32, (bk, bk), 0)
    pc = lax.broadcasted_iota(jnp.int32, (bk, bk), 1)
    tri_p = (pr >= pc).astype(BF16)

    def cache_block(kref, vref, c, accs):
        y = jnp.concatenate([_dot(qs[h], kref[h].astype(BF16)) for h in range(SB_HEADS)], axis=0) * LOG2E
        cum = _suffix_sums(_softplus2(y), tri_p)
        w = _sb_weights(y, cum, c, None)
        accs = [accs[h] + _dot_nt(w[h * lq:(h + 1) * lq], vref[h].astype(BF16)) for h in range(SB_HEADS)]
        return c + cum[:, 0:1], accs

    c, accs = cache_block(kl_ref, vl_ref, c, accs)

    def fetch(j):
        s0 = pl.multiple_of(j * bk, bk)
        return (pltpu.make_async_copy(kc_hbm.at[layer, b, :, :, pl.ds(s0, bk)], kbuf, sem.at[0]),
                pltpu.make_async_copy(vc_hbm.at[layer, b, :, :, pl.ds(s0, bk)], vbuf, sem.at[1]))

    def visit(st):
        j, c, accs = st[0], st[2], list(st[3:])
        copies = fetch(j)
        for cp in copies:
            cp.start()
        for cp in copies:
            cp.wait()
        c, accs = cache_block(kbuf, vbuf, c, accs)
        return (j - 1, _unfinished(c), c) + tuple(accs)

    st = lax.while_loop(_more, visit, (n_past - 2, _unfinished(c), c) + tuple(accs))
    o_ref[...] = jnp.concatenate(st[3:], axis=1).astype(o_ref.dtype)


def _attention_cached(q, k, v, cache_kt, cache_vt, layer, *, bk=256):
    bsz, lq, _ = q.shape
    past = cache_kt.shape[4]
    n_past = past // bk
    tok = pl.BlockSpec((None, lq, SB_WIDTH), lambda b: (b, 0, 0))
    last = pl.BlockSpec((None, None, SB_HEADS, SB_HEAD_DIM, bk), lambda b: (layer, b, 0, 0, n_past - 1))
    hbm = pl.BlockSpec(memory_space=pl.ANY)
    return pl.pallas_call(
        functools.partial(_attn_cached_kernel, layer=layer, lq=lq, bk=bk, n_past=n_past),
        grid=(bsz,),
        in_specs=[tok, tok, tok, last, last, hbm, hbm],
        out_specs=tok,
        out_shape=jax.ShapeDtypeStruct((bsz, lq, SB_WIDTH), BF16),
        scratch_shapes=[pltpu.VMEM((SB_HEADS, SB_HEAD_DIM, bk), F32), pltpu.VMEM((SB_HEADS, SB_HEAD_DIM, bk), F32),
                        pltpu.SemaphoreType.DMA((2,))],
        compiler_params=pltpu.CompilerParams(
            dimension_semantics=("parallel",), vmem_limit_bytes=VMEM_LIMIT),
        name="sb_attention_cached",
    )(q, k, v, cache_kt, cache_vt, cache_kt, cache_vt)


def _gelu_tanh(x):
    return 0.5 * x * (1.0 + jnp.tanh(math.sqrt(2.0 / math.pi) * (x + 0.044715 * (x * x * x))))


def _ssm_kernel(u_ref, s0re_ref, s0im_ref, lbre_ref, lbim_ref, bre_ref, bim_ref, cre_ref, cimn_ref,
                d_ref, wglu_ref, o_ref, sfre_ref, sfim_ref, sre, sim, st_re, st_im, tmaj, *, bsz, tc, lw):
    step = pl.program_id(0)

    @pl.when(step == 0)
    def _():
        st_re[...] = s0re_ref[...]
        st_im[...] = s0im_ref[...]

    for b in range(bsz):
        for m in range(SSM_SLABS):
            tmaj[m, pl.ds(b, tc, stride=bsz), :] = u_ref[b, :, m * LANES:(m + 1) * LANES]
    u = jnp.concatenate([tmaj[m] for m in range(SSM_SLABS)], axis=1)
    ub = u.astype(BF16)
    for m in range(SSM_SLABS):
        um = ub[:, m * LANES:(m + 1) * LANES]
        sre[:, m * SLAB_STATE:(m + 1) * SLAB_STATE] = _dot(um, bre_ref[m])
        sim[:, m * SLAB_STATE:(m + 1) * SLAB_STATE] = _dot(um, bim_ref[m])

    for lc in range(SSM_LANES // lw):
        lanes = slice(lc * lw, (lc + 1) * lw)
        lr = jnp.broadcast_to(lbre_ref[:, lanes], (8, lw))
        li = jnp.broadcast_to(lbim_ref[:, lanes], (8, lw))
        for sb in range(bsz // 8):
            subl = slice(sb * 8, (sb + 1) * 8)

            def body(t, carry):
                sr, si = carry
                r0 = pl.multiple_of(t * bsz + sb * 8, 8)
                nr = lr * sr - li * si + sre[pl.ds(r0, 8), lanes]
                ni = lr * si + li * sr + sim[pl.ds(r0, 8), lanes]
                sre[pl.ds(r0, 8), lanes] = nr
                sim[pl.ds(r0, 8), lanes] = ni
                return nr, ni

            sr, si = lax.fori_loop(0, tc, body, (st_re[subl, lanes], st_im[subl, lanes]), unroll=4)
            st_re[subl, lanes] = sr
            st_im[subl, lanes] = si

    ys = []
    for m in range(SSM_SLABS):
        slab = slice(m * SLAB_STATE, (m + 1) * SLAB_STATE)
        ys.append(_dot(sre[:, slab].astype(BF16), cre_ref[m]) + _dot(sim[:, slab].astype(BF16), cimn_ref[m]))
    y = jnp.concatenate(ys, axis=1) + d_ref[...] * u
    y = _gelu_tanh(y)
    o = y * _sigmoid(_dot(y.astype(BF16), wglu_ref[...]))
    for m in range(SSM_SLABS):
        tmaj[m] = o[:, m * LANES:(m + 1) * LANES]
    for b in range(bsz):
        for m in range(SSM_SLABS):
            o_ref[b, :, m * LANES:(m + 1) * LANES] = tmaj[m, pl.ds(b, tc, stride=bsz), :].astype(o_ref.dtype)

    @pl.when(step == pl.num_programs(0) - 1)
    def _():
        sfre_ref[...] = st_re[...]
        sfim_ref[...] = st_im[...]


def _ssm(u, s0_re, s0_im, prm, *, tc, lw=512):
    bsz, L, _ = u.shape
    blk = tc * bsz
    row_spec = pl.BlockSpec((bsz, tc, SSM_WIDTH), lambda s: (0, s, 0))
    st_shape = (bsz, SSM_LANES)
    return pl.pallas_call(
        functools.partial(_ssm_kernel, bsz=bsz, tc=tc, lw=lw),
        grid=(L // tc,),
        in_specs=[row_spec, _const_spec(st_shape), _const_spec(st_shape),
                  _const_spec((1, SSM_LANES)), _const_spec((1, SSM_LANES)),
                  _const_spec((SSM_SLABS, LANES, SLAB_STATE)), _const_spec((SSM_SLABS, LANES, SLAB_STATE)),
                  _const_spec((SSM_SLABS, SLAB_STATE, LANES)), _const_spec((SSM_SLABS, SLAB_STATE, LANES)),
                  _const_spec((1, SSM_WIDTH)), _const_spec((SSM_WIDTH, SSM_WIDTH))],
        out_specs=[row_spec, _const_spec(st_shape), _const_spec(st_shape)],
        out_shape=[jax.ShapeDtypeStruct((bsz, L, SSM_WIDTH), BF16),
                   jax.ShapeDtypeStruct(st_shape, F32), jax.ShapeDtypeStruct(st_shape, F32)],
        scratch_shapes=[pltpu.VMEM((blk, SSM_LANES), F32), pltpu.VMEM((blk, SSM_LANES), F32),
                        pltpu.VMEM(st_shape, F32), pltpu.VMEM(st_shape, F32),
                        pltpu.VMEM((SSM_SLABS, tc * bsz, LANES), F32)],
        compiler_params=pltpu.CompilerParams(
            dimension_semantics=("arbitrary",), vmem_limit_bytes=VMEM_LIMIT),
        name="s5_scan",
    )(u, s0_re, s0_im, prm["lb_re"], prm["lb_im"], prm["bd_re"], prm["bd_im"],
      prm["cd_re"], prm["cd_im_neg"], prm["d"], prm["w_glu"])


def _ssm_params(a_re, a_im, log_dt, b_re, b_im, c_re, c_im, d, w_glu):
    dt = jnp.exp(log_dt)[:, None]
    mag = jnp.exp(a_re * dt)
    lb_re = mag * jnp.cos(a_im * dt)
    lb_im = mag * jnp.sin(a_im * dt)
    den = a_re * a_re + a_im * a_im
    nr, ni = lb_re - 1.0, lb_im
    f_re = (nr * a_re + ni * a_im) / den
    f_im = (ni * a_re - nr * a_im) / den
    bb_re = f_re[:, :, None] * b_re - f_im[:, :, None] * b_im
    bb_im = f_re[:, :, None] * b_im + f_im[:, :, None] * b_re
    gps = SSM_GROUPS // SSM_SLABS
    eye = jnp.eye(gps, dtype=F32)

    def b_slabs(bb):
        t = bb.transpose(0, 2, 1).reshape(SSM_SLABS, gps, SSM_GROUP, SSM_STATE)
        return jnp.einsum("mgcp,gh->mgchp", t, eye).reshape(SSM_SLABS, LANES, SLAB_STATE).astype(BF16)

    def c_slabs(c):
        t = c.transpose(0, 2, 1).reshape(SSM_SLABS, gps, SSM_STATE, SSM_GROUP)
        return jnp.einsum("mgpc,gh->mgphc", t, eye).reshape(SSM_SLABS, SLAB_STATE, LANES).astype(BF16)

    return dict(lb_re=lb_re.reshape(1, SSM_LANES), lb_im=lb_im.reshape(1, SSM_LANES),
                bd_re=b_slabs(bb_re), bd_im=b_slabs(bb_im),
                cd_re=c_slabs(c_re), cd_im_neg=c_slabs(-c_im),
                d=d.reshape(1, SSM_WIDTH), w_glu=w_glu.astype(BF16))


def _post_kernel(x_ref, oa_ref, os_ref, ga_ref, gs_ref, p_ref,
                 wba_ref, wbs_ref, wout_ref, wg_ref, wu_ref, wd_ref, wpg_ref, wpp_ref,
                 n_mix_post, n_ffn_pre, n_ffn_post, n_ple_pre, n_ple_post,
                 y_ref, *, nb, tm, ff_chunk):
    rows = nb * tm
    x = x_ref[...].reshape(rows, D_MODEL)
    oa = oa_ref[...].reshape(rows, SB_WIDTH)
    os_ = os_ref[...].reshape(rows, SSM_WIDTH)
    merged = (_sigmoid(ga_ref[...].reshape(rows, D_MODEL)) * _dot(oa, wba_ref[...])
              + _sigmoid(gs_ref[...].reshape(rows, D_MODEL)) * _dot(os_, wbs_ref[...]))
    x = x + _rms(_dot(merged.astype(BF16), wout_ref[...]), n_mix_post[...])

    f = _rms(x, n_ffn_pre[...]).astype(BF16)
    ff = jnp.zeros((rows, D_MODEL), F32)
    for c in range(D_FF // ff_chunk):
        cols = slice(c * ff_chunk, (c + 1) * ff_chunk)
        g = _dot(f, wg_ref[:, cols])
        a = (g * _sigmoid(g)) * _dot(f, wu_ref[:, cols])
        ff = ff + _dot(a.astype(BF16), wd_ref[cols, :])
    x = x + _rms(ff, n_ffn_post[...])

    gate = _sigmoid(_dot(_rms(x, n_ple_pre[...]).astype(BF16), wpg_ref[...]))
    pe = gate * _dot(p_ref[...].reshape(rows, PLE_DIM).astype(BF16), wpp_ref[...])
    y_ref[...] = (x + _rms(pe, n_ple_post[...])).reshape(nb, tm, D_MODEL)


def _post(x, o_attn, o_ssm, g_attn, g_ssm, p, W, *, nb, tm, ff_chunk=256):
    bsz, L, _ = x.shape
    grid = (bsz // nb, L // tm)
    tok = lambda width: pl.BlockSpec((nb, tm, width), lambda b, i: (b, i, 0))
    vec = _const_spec((1, D_MODEL))
    return pl.pallas_call(
        functools.partial(_post_kernel, nb=nb, tm=tm, ff_chunk=ff_chunk),
        grid=grid,
        in_specs=[tok(D_MODEL), tok(SB_WIDTH), tok(SSM_WIDTH),
                  tok(D_MODEL), tok(D_MODEL), tok(PLE_DIM),
                  _const_spec((SB_WIDTH, D_MODEL)), _const_spec((SSM_WIDTH, D_MODEL)),
                  _const_spec((D_MODEL, D_MODEL)),
                  _const_spec((D_MODEL, D_FF)), _const_spec((D_MODEL, D_FF)), _const_spec((D_FF, D_MODEL)),
                  _const_spec((D_MODEL, D_MODEL)), _const_spec((PLE_DIM, D_MODEL)),
                  vec, vec, vec, vec, vec],
        out_specs=tok(D_MODEL),
        out_shape=jax.ShapeDtypeStruct((bsz, L, D_MODEL), F32),
        compiler_params=pltpu.CompilerParams(
            dimension_semantics=("parallel", "parallel"), vmem_limit_bytes=VMEM_LIMIT),
        name="post",
    )(x, o_attn, o_ssm, g_attn, g_ssm, p,
      W["w_branch_attn"], W["w_branch_ssm"], W["w_out"], W["w_ffn_gate"], W["w_ffn_up"], W["w_ffn_down"],
      W["w_ple_gate"], W["w_ple_proj"],
      W["norm_mix_post"], W["norm_ffn_pre"], W["norm_ffn_post"], W["norm_ple_pre"], W["norm_ple_post"])


def _layer(x, p, cache, s_re0, s_im0, W, *, nb, tm, bq, tc, post_nb, post_tm):
    bsz, L, _ = x.shape
    q, k, v, u, g_attn, g_ssm = _inproj(x, W["norm_mix_pre"], W["w_in"], W["w_kv_t"], nb=nb, tm=tm,
                                           kv_t=cache is None)
    if cache is None:
        o_attn = _attention(q, k, v, bq=bq)
        k, v = (jnp.transpose(a, (0, 3, 1, 2)) for a in (k, v))
    else:
        o_attn = _attention_cached(q, k, v, *cache)
    o_ssm, s_re, s_im = _ssm(u, s_re0.reshape(bsz, SSM_LANES), s_im0.reshape(bsz, SSM_LANES), W["ssm"], tc=tc)
    y = _post(x, o_attn, o_ssm, g_attn, g_ssm, p, W, nb=post_nb, tm=post_tm)
    heads = (bsz, L, SB_HEADS, SB_HEAD_DIM)
    state = (bsz, SSM_GROUPS, SSM_STATE)
    return y, k.reshape(heads), v.reshape(heads), s_re.reshape(state), s_im.reshape(state)


def kernel(x_prompt, x_sample, cache_k, cache_v, state_ssm_re, state_ssm_im, p_prompt, p_sample, norm_mix_pre, norm_mix_post, w_in, ssm_a_re, ssm_a_im, ssm_log_dt, ssm_b_re, ssm_b_im, ssm_c_re, ssm_c_im, ssm_d, w_glu, w_branch_attn, w_branch_ssm, w_out, norm_ffn_pre, norm_ffn_post, w_ffn_gate, w_ffn_up, w_ffn_down, norm_ple_pre, norm_ple_post, w_ple_gate, w_ple_proj):
    depth = w_in.shape[0]
    yp, ys = x_prompt, x_sample
    outs = [[] for _ in range(8)]
    cache_kt, cache_vt = (jnp.transpose(c, (0, 1, 3, 4, 2)) for c in (cache_k, cache_v))
    for i in range(depth):
        W = dict(
            norm_mix_pre=norm_mix_pre[i][None], norm_mix_post=norm_mix_post[i][None],
            norm_ffn_pre=norm_ffn_pre[i][None], norm_ffn_post=norm_ffn_post[i][None],
            norm_ple_pre=norm_ple_pre[i][None], norm_ple_post=norm_ple_post[i][None],
            w_in=w_in[i].astype(BF16), w_kv_t=w_in[i][:, SB_WIDTH:3 * SB_WIDTH].T.astype(BF16),
            w_branch_attn=w_branch_attn[i].astype(BF16),
            w_branch_ssm=w_branch_ssm[i].astype(BF16), w_out=w_out[i].astype(BF16),
            w_ffn_gate=w_ffn_gate[i].astype(BF16), w_ffn_up=w_ffn_up[i].astype(BF16),
            w_ffn_down=w_ffn_down[i].astype(BF16), w_ple_gate=w_ple_gate[i].astype(BF16),
            w_ple_proj=w_ple_proj[i].astype(BF16),
            ssm=_ssm_params(ssm_a_re[i], ssm_a_im[i], ssm_log_dt[i], ssm_b_re[i], ssm_b_im[i],
                            ssm_c_re[i], ssm_c_im[i], ssm_d[i], w_glu[i]))
        bp, lp = yp.shape[0], yp.shape[1]
        bs, ls = ys.shape[0], ys.shape[1]
        zero_state = jnp.zeros((bp, SSM_GROUPS, SSM_STATE), F32)
        yp, kp, vp, srp, sip = _layer(
            yp, p_prompt[i], None, zero_state, zero_state, W,
            nb=1, tm=min(512, lp), bq=min(256, lp), tc=512 // bp, post_nb=1, post_tm=min(512, lp))
        ys, kn, vn, srs, sis = _layer(
            ys, p_sample[i], (cache_kt, cache_vt, i), state_ssm_re[i], state_ssm_im[i], W,
            nb=512 // ls, tm=ls, bq=ls, tc=512 // bs, post_nb=512 // ls, post_tm=ls)
        for lst, val in zip(outs, (kp, vp, srp, sip, kn, vn, srs, sis)):
            lst.append(val)
    return (yp, ys) + tuple(jnp.stack(o) for o in outs)
```

```python
import functools
import math

import jax
import jax.numpy as jnp
from jax import lax
from jax.experimental import pallas as pl
from jax.experimental.pallas import tpu as pltpu

F32 = jnp.float32
BF16 = jnp.bfloat16

D_MODEL = 1024
PLE_DIM = 256
SB_WIDTH = 512
SB_HEAD_DIM = 64
SB_HEADS = 8
SSM_WIDTH = 512
SSM_GROUP = 16
SSM_GROUPS = 32
SSM_STATE = 64
SSM_LANES = SSM_GROUPS * SSM_STATE
D_FF = 2816
IN_WIDTH = 3 * SB_WIDTH + SSM_WIDTH + 2 * D_MODEL
RMS_EPS = 1e-6

LANES = 128
HEADS_PER_BLOCK = LANES // SB_HEAD_DIM
SSM_SLABS = SSM_WIDTH // LANES
SLAB_STATE = SSM_LANES // SSM_SLABS
VMEM_LIMIT = 56 * 1024 * 1024


def _const_spec(shape):
    nd = len(shape)
    return pl.BlockSpec(shape, lambda *_: (0,) * nd, pipeline_mode=pl.Buffered(1))


def _rms(x, gain):
    ms = jnp.mean(x * x, axis=-1, keepdims=True)
    return x * lax.rsqrt(ms + RMS_EPS) * gain


def _sigmoid(x):
    return 1.0 / (1.0 + jnp.exp(-x))


def _dot(a, b):
    return jnp.dot(a, b, preferred_element_type=F32)


def _dot_nt(a, b):
    return lax.dot_general(a, b, (((1,), (1,)), ((), ())), preferred_element_type=F32)


def _inproj_kernel(*refs, nb, tm, kv_t):
    if kv_t:
        x_ref, gain_ref, w_ref, wkvt_ref, q_ref, k_ref, v_ref, u_ref, ga_ref, gs_ref = refs
    else:
        x_ref, gain_ref, w_ref, q_ref, k_ref, v_ref, u_ref, ga_ref, gs_ref = refs
    x = x_ref[...].reshape(nb * tm, D_MODEL)
    h = _rms(x, gain_ref[...]).astype(BF16)

    def proj(lo, width):
        return _dot(h, w_ref[:, lo:lo + width])

    q_ref[...] = (proj(0, SB_WIDTH) * (SB_HEAD_DIM ** -0.5)).astype(BF16).reshape(nb, tm, SB_WIDTH)
    if kv_t:
        k_ref[...] = _dot_nt(wkvt_ref[0:SB_WIDTH, :], h).reshape(SB_HEADS, SB_HEAD_DIM, tm)
        v_ref[...] = _dot_nt(wkvt_ref[SB_WIDTH:2 * SB_WIDTH, :], h).reshape(SB_HEADS, SB_HEAD_DIM, tm)
    else:
        k_ref[...] = proj(SB_WIDTH, SB_WIDTH).reshape(nb, tm, SB_WIDTH)
        v_ref[...] = proj(2 * SB_WIDTH, SB_WIDTH).reshape(nb, tm, SB_WIDTH)
    u_ref[...] = proj(3 * SB_WIDTH, SSM_WIDTH).reshape(nb, tm, SSM_WIDTH)
    off = 3 * SB_WIDTH + SSM_WIDTH
    ga_ref[...] = proj(off, D_MODEL).reshape(nb, tm, D_MODEL)
    gs_ref[...] = proj(off + D_MODEL, D_MODEL).reshape(nb, tm, D_MODEL)


def _inproj(x, gain, w_in, w_kv_t, *, nb, tm, kv_t):
    bsz, L, _ = x.shape
    grid = (bsz // nb, L // tm)
    tok = lambda width: pl.BlockSpec((nb, tm, width), lambda b, i: (b, i, 0))
    in_specs = [tok(D_MODEL), _const_spec((1, D_MODEL)), _const_spec((D_MODEL, IN_WIDTH))]
    args = [x, gain, w_in]
    if kv_t:
        assert nb == 1
        in_specs.append(_const_spec((2 * SB_WIDTH, D_MODEL)))
        args.append(w_kv_t)
        kv_spec = pl.BlockSpec((None, SB_HEADS, SB_HEAD_DIM, tm), lambda b, i: (b, 0, 0, i))
        kv_shape = jax.ShapeDtypeStruct((bsz, SB_HEADS, SB_HEAD_DIM, L), F32)
    else:
        kv_spec = tok(SB_WIDTH)
        kv_shape = jax.ShapeDtypeStruct((bsz, L, SB_WIDTH), F32)
    return pl.pallas_call(
        functools.partial(_inproj_kernel, nb=nb, tm=tm, kv_t=kv_t),
        grid=grid,
        in_specs=in_specs,
        out_specs=[tok(SB_WIDTH), kv_spec, kv_spec, tok(SSM_WIDTH), tok(D_MODEL), tok(D_MODEL)],
        out_shape=[jax.ShapeDtypeStruct((bsz, L, SB_WIDTH), BF16), kv_shape, kv_shape,
                   jax.ShapeDtypeStruct((bsz, L, SSM_WIDTH), F32),
                   jax.ShapeDtypeStruct((bsz, L, D_MODEL), F32),
                   jax.ShapeDtypeStruct((bsz, L, D_MODEL), F32)],
        compiler_params=pltpu.CompilerParams(
            dimension_semantics=("parallel", "parallel"), vmem_limit_bytes=VMEM_LIMIT),
        name="inproj",
    )(*args)


LOG2E = 1.0 / math.log(2.0)
SKIP_LOG2 = 105.0 * LOG2E


SOFTPLUS2_LINEAR = 100.0


def _softplus2(y):
    return jnp.maximum(y, jnp.log2(1.0 + jnp.exp2(jnp.minimum(y, SOFTPLUS2_LINEAR))))


def _suffix_sums(sp, tri):
    return _dot(sp.astype(BF16), tri)


def _sb_weights(y, cum, c, visible):
    w = jnp.exp2(y - cum - c)
    if visible is not None:
        w = jnp.where(visible, w, 0.0)
    return w.astype(BF16)


def _unfinished(*cs):
    m = jnp.min(cs[0])
    for c in cs[1:]:
        m = jnp.minimum(m, jnp.min(c))
    return (m < SKIP_LOG2).astype(jnp.int32)


def _more(st):
    return (st[0] >= 0) & (st[1] > 0)


def _attn_kernel(q_ref, k_ref, v_ref, o_ref, c_ref, acc_ref, *, bq):
    i = pl.program_id(1)
    n_pairs = SB_WIDTH // LANES
    lane = lax.broadcasted_iota(jnp.int32, (bq, LANES), 1)
    row = lax.broadcasted_iota(jnp.int32, (bq, bq), 0)
    col = lax.broadcasted_iota(jnp.int32, (bq, bq), 1)
    tri = (row >= col).astype(BF16)
    diag_visible = col < row

    in_head = [(lane >= h * SB_HEAD_DIM) & (lane < (h + 1) * SB_HEAD_DIM) for h in range(HEADS_PER_BLOCK)]
    qs = []
    for p in range(n_pairs):
        qp = q_ref[:, p * LANES:(p + 1) * LANES]
        qs += [jnp.where(m, qp, jnp.zeros_like(qp)) for m in in_head]

    def visit(j, visible, first):
        s0 = pl.multiple_of(j * bq, bq)
        kts, vts = [], []
        for p in range(n_pairs):
            heads = slice(p * HEADS_PER_BLOCK, (p + 1) * HEADS_PER_BLOCK)
            kts.append(k_ref[heads, :, pl.ds(s0, bq)].reshape(LANES, bq).astype(BF16))
            vts.append(v_ref[heads, :, pl.ds(s0, bq)].reshape(LANES, bq).astype(BF16))
        ys = [_dot(qs[h], kts[h // HEADS_PER_BLOCK]) * LOG2E for h in range(SB_HEADS)]
        sps = [_softplus2(y) if visible is None else jnp.where(visible, _softplus2(y), 0.0) for y in ys]
        cums = [_suffix_sums(sp, tri) for sp in sps]
        cs_old = [0.0 if first else c_ref[h] for h in range(SB_HEADS)]
        ws = [_sb_weights(ys[h], cums[h], cs_old[h], visible) for h in range(SB_HEADS)]
        accs = [_dot_nt(ws[h], vts[h // HEADS_PER_BLOCK]) for h in range(SB_HEADS)]
        cs = [cums[h][:, 0:1] + cs_old[h] for h in range(SB_HEADS)]
        for h in range(SB_HEADS):
            acc_ref[h] = accs[h] if first else acc_ref[h] + accs[h]
            c_ref[h] = cs[h]
        return _unfinished(*cs)

    go = visit(i, diag_visible, True)
    lax.while_loop(_more, lambda st: (st[0] - 1, visit(st[0], None, False)), (i - 1, go))
    for p in range(n_pairs):
        o_ref[:, p * LANES:(p + 1) * LANES] = jnp.where(
            in_head[0], acc_ref[p * HEADS_PER_BLOCK], acc_ref[p * HEADS_PER_BLOCK + 1]).astype(o_ref.dtype)


def _attention(q, k_t, v_t, *, bq):
    bsz, L, _ = q.shape
    qspec = pl.BlockSpec((None, bq, SB_WIDTH), lambda b, i: (b, i, 0))
    kvspec = pl.BlockSpec((None, SB_HEADS, SB_HEAD_DIM, L), lambda b, i: (b, 0, 0, 0))
    return pl.pallas_call(
        functools.partial(_attn_kernel, bq=bq),
        grid=(bsz, L // bq),
        in_specs=[qspec, kvspec, kvspec],
        out_specs=qspec,
        out_shape=jax.ShapeDtypeStruct((bsz, L, SB_WIDTH), BF16),
        scratch_shapes=[pltpu.VMEM((SB_HEADS, bq, 1), F32), pltpu.VMEM((SB_HEADS, bq, LANES), F32)],
        compiler_params=pltpu.CompilerParams(
            dimension_semantics=("parallel", "parallel"), vmem_limit_bytes=VMEM_LIMIT),
        name="sb_attention",
    )(q, k_t, v_t)


def _attn_cached_kernel(q_ref, k_ref, v_ref, kl_ref, vl_ref, kc_hbm, vc_hbm, o_ref, kbuf, vbuf, sem,
                        *, layer, lq, bk, n_past):
    b = pl.program_id(0)
    hd = SB_HEAD_DIM
    q = q_ref[...]
    kn = k_ref[...].astype(BF16)
    vn = v_ref[...].astype(BF16)
    head = lambda x, h: x[:, h * hd:(h + 1) * hd]
    qs = [head(q, h) for h in range(SB_HEADS)]
    rows = SB_HEADS * lq

    r = lax.broadcasted_iota(jnp.int32, (rows, lq), 0) % lq
    s = lax.broadcasted_iota(jnp.int32, (rows, lq), 1)
    visible = s < r
    tr = lax.broadcasted_iota(jnp.int32, (lq, lq), 0)
    tc = lax.broadcasted_iota(jnp.int32, (lq, lq), 1)
    yd = jnp.concatenate([_dot_nt(qs[h], head(kn, h)) for h in range(SB_HEADS)], axis=0) * LOG2E
    cumd = _suffix_sums(jnp.where(visible, _softplus2(yd), 0.0), (tr >= tc).astype(BF16))
    wd = _sb_weights(yd, cumd, 0.0, visible)
    accs = [_dot(wd[h * lq:(h + 1) * lq], head(vn, h)) for h in range(SB_HEADS)]
    c = cumd[:, 0:1]

    pr = lax.broadcasted_iota(jnp.int32, (bk, bk), 0)
    pc = lax.broadcasted_iota(jnp.int32, (bk, bk), 1)
    tri_p = (pr >= pc).astype(BF16)

    def cache_block(kref, vref, c, accs):
        y = jnp.concatenate([_dot(qs[h], kref[h].astype(BF16)) for h in range(SB_HEADS)], axis=0) * LOG2E
        cum = _suffix_sums(_softplus2(y), tri_p)
        w = _sb_weights(y, cum, c, None)
        accs = [accs[h] + _dot_nt(w[h * lq:(h + 1) * lq], vref[h].astype(BF16)) for h in range(SB_HEADS)]
        return c + cum[:, 0:1], accs

    c, accs = cache_block(kl_ref, vl_ref, c, accs)

    def fetch(j):
        s0 = pl.multiple_of(j * bk, bk)
        return (pltpu.make_async_copy(kc_hbm.at[layer, b, :, :, pl.ds(s0, bk)], kbuf, sem.at[0]),
                pltpu.make_async_copy(vc_hbm.at[layer, b, :, :, pl.ds(s0, bk)], vbuf, sem.at[1]))

    def visit(st):
        j, c, accs = st[0], st[2], list(st[3:])
        copies = fetch(j)
        for cp in copies:
            cp.start()
        for cp in copies:
            cp.wait()
        c, accs = cache_block(kbuf, vbuf, c, accs)
        return (j - 1, _unfinished(c), c) + tuple(accs)

    st = lax.while_loop(_more, visit, (n_past - 2, _unfinished(c), c) + tuple(accs))
    o_ref[...] = jnp.concatenate(st[3:], axis=1).astype(o_ref.dtype)


def _attention_cached(q, k, v, cache_kt, cache_vt, layer, *, bk=256):
    bsz, lq, _ = q.shape
    past = cache_kt.shape[4]
    n_past = past // bk
    tok = pl.BlockSpec((None, lq, SB_WIDTH), lambda b: (b, 0, 0))
    last = pl.BlockSpec((None, None, SB_HEADS, SB_HEAD_DIM, bk), lambda b: (layer, b, 0, 0, n_past - 1))
    hbm = pl.BlockSpec(memory_space=pl.ANY)
    return pl.pallas_call(
        functools.partial(_attn_cached_kernel, layer=layer, lq=lq, bk=bk, n_past=n_past),
        grid=(bsz,),
        in_specs=[tok, tok, tok, last, last, hbm, hbm],
        out_specs=tok,
        out_shape=jax.ShapeDtypeStruct((bsz, lq, SB_WIDTH), BF16),
        scratch_shapes=[pltpu.VMEM((SB_HEADS, SB_HEAD_DIM, bk), F32), pltpu.VMEM((SB_HEADS, SB_HEAD_DIM, bk), F32),
                        pltpu.SemaphoreType.DMA((2,))],
        compiler_params=pltpu.CompilerParams(
            dimension_semantics=("parallel",), vmem_limit_bytes=VMEM_LIMIT),
        name="sb_attention_cached",
    )(q, k, v, cache_kt, cache_vt, cache_kt, cache_vt)


def _gelu_tanh(x):
    return 0.5 * x * (1.0 + jnp.tanh(math.sqrt(2.0 / math.pi) * (x + 0.044715 * (x * x * x))))


def _ssm_kernel(u_ref, s0re_ref, s0im_ref, lbre_ref, lbim_ref, bre_ref, bim_ref, cre_ref, cimn_ref,
                d_ref, wglu_ref, o_ref, sfre_ref, sfim_ref, sre, sim, st_re, st_im, tmaj, *, bsz, tc, lw):
    step = pl.program_id(0)

    @pl.when(step == 0)
    def _():
        st_re[...] = s0re_ref[...]
        st_im[...] = s0im_ref[...]

    for b in range(bsz):
        for m in range(SSM_SLABS):
            tmaj[m, pl.ds(b, tc, stride=bsz), :] = u_ref[b, :, m * LANES:(m + 1) * LANES]
    u = jnp.concatenate([tmaj[m] for m in range(SSM_SLABS)], axis=1)
    ub = u.astype(BF16)
    for m in range(SSM_SLABS):
        um = ub[:, m * LANES:(m + 1) * LANES]
        sre[:, m * SLAB_STATE:(m + 1) * SLAB_STATE] = _dot(um, bre_ref[m])
        sim[:, m * SLAB_STATE:(m + 1) * SLAB_STATE] = _dot(um, bim_ref[m])

    for lc in range(SSM_LANES // lw):
        lanes = slice(lc * lw, (lc + 1) * lw)
        lr = jnp.broadcast_to(lbre_ref[:, lanes], (8, lw))
        li = jnp.broadcast_to(lbim_ref[:, lanes], (8, lw))
        for sb in range(bsz // 8):
            subl = slice(sb * 8, (sb + 1) * 8)

            def body(t, carry):
                sr, si = carry
                r0 = pl.multiple_of(t * bsz + sb * 8, 8)
                nr = lr * sr - li * si + sre[pl.ds(r0, 8), lanes]
                ni = lr * si + li * sr + sim[pl.ds(r0, 8), lanes]
                sre[pl.ds(r0, 8), lanes] = nr
                sim[pl.ds(r0, 8), lanes] = ni
                return nr, ni

            sr, si = lax.fori_loop(0, tc, body, (st_re[subl, lanes], st_im[subl, lanes]), unroll=4)
            st_re[subl, lanes] = sr
            st_im[subl, lanes] = si

    ys = []
    for m in range(SSM_SLABS):
        slab = slice(m * SLAB_STATE, (m + 1) * SLAB_STATE)
        ys.append(_dot(sre[:, slab].astype(BF16), cre_ref[m]) + _dot(sim[:, slab].astype(BF16), cimn_ref[m]))
    y = jnp.concatenate(ys, axis=1) + d_ref[...] * u
    y = _gelu_tanh(y)
    o = y * _sigmoid(_dot(y.astype(BF16), wglu_ref[...]))
    for m in range(SSM_SLABS):
        tmaj[m] = o[:, m * LANES:(m + 1) * LANES]
    for b in range(bsz):
        for m in range(SSM_SLABS):
            o_ref[b, :, m * LANES:(m + 1) * LANES] = tmaj[m, pl.ds(b, tc, stride=bsz), :].astype(o_ref.dtype)

    @pl.when(step == pl.num_programs(0) - 1)
    def _():
        sfre_ref[...] = st_re[...]
        sfim_ref[...] = st_im[...]


def _ssm(u, s0_re, s0_im, prm, *, tc, lw=512):
    bsz, L, _ = u.shape
    blk = tc * bsz
    row_spec = pl.BlockSpec((bsz, tc, SSM_WIDTH), lambda s: (0, s, 0))
    st_shape = (bsz, SSM_LANES)
    return pl.pallas_call(
        functools.partial(_ssm_kernel, bsz=bsz, tc=tc, lw=lw),
        grid=(L // tc,),
        in_specs=[row_spec, _const_spec(st_shape), _const_spec(st_shape),
                  _const_spec((1, SSM_LANES)), _const_spec((1, SSM_LANES)),
                  _const_spec((SSM_SLABS, LANES, SLAB_STATE)), _const_spec((SSM_SLABS, LANES, SLAB_STATE)),
                  _const_spec((SSM_SLABS, SLAB_STATE, LANES)), _const_spec((SSM_SLABS, SLAB_STATE, LANES)),
                  _const_spec((1, SSM_WIDTH)), _const_spec((SSM_WIDTH, SSM_WIDTH))],
        out_specs=[row_spec, _const_spec(st_shape), _const_spec(st_shape)],
        out_shape=[jax.ShapeDtypeStruct((bsz, L, SSM_WIDTH), BF16),
                   jax.ShapeDtypeStruct(st_shape, F32), jax.ShapeDtypeStruct(st_shape, F32)],
        scratch_shapes=[pltpu.VMEM((blk, SSM_LANES), F32), pltpu.VMEM((blk, SSM_LANES), F32),
                        pltpu.VMEM(st_shape, F32), pltpu.VMEM(st_shape, F32),
                        pltpu.VMEM((SSM_SLABS, tc * bsz, LANES), F32)],
        compiler_params=pltpu.CompilerParams(
            dimension_semantics=("arbitrary",), vmem_limit_bytes=VMEM_LIMIT),
        name="s5_scan",
    )(u, s0_re, s0_im, prm["lb_re"], prm["lb_im"], prm["bd_re"], prm["bd_im"],
      prm["cd_re"], prm["cd_im_neg"], prm["d"], prm["w_glu"])


def _ssm_params(a_re, a_im, log_dt, b_re, b_im, c_re, c_im, d, w_glu):
    dt = jnp.exp(log_dt)[:, None]
    mag = jnp.exp(a_re * dt)
    lb_re = mag * jnp.cos(a_im * dt)
    lb_im = mag * jnp.sin(a_im * dt)
    den = a_re * a_re + a_im * a_im
    nr, ni = lb_re - 1.0, lb_im
    f_re = (nr * a_re + ni * a_im) / den
    f_im = (ni * a_re - nr * a_im) / den
    bb_re = f_re[:, :, None] * b_re - f_im[:, :, None] * b_im
    bb_im = f_re[:, :, None] * b_im + f_im[:, :, None] * b_re
    gps = SSM_GROUPS // SSM_SLABS
    eye = jnp.eye(gps, dtype=F32)

    def b_slabs(bb):
        t = bb.transpose(0, 2, 1).reshape(SSM_SLABS, gps, SSM_GROUP, SSM_STATE)
        return jnp.einsum("mgcp,gh->mgchp", t, eye).reshape(SSM_SLABS, LANES, SLAB_STATE).astype(BF16)

    def c_slabs(c):
        t = c.transpose(0, 2, 1).reshape(SSM_SLABS, gps, SSM_STATE, SSM_GROUP)
        return jnp.einsum("mgpc,gh->mgphc", t, eye).reshape(SSM_SLABS, SLAB_STATE, LANES).astype(BF16)

    return dict(lb_re=lb_re.reshape(1, SSM_LANES), lb_im=lb_im.reshape(1, SSM_LANES),
                bd_re=b_slabs(bb_re), bd_im=b_slabs(bb_im),
                cd_re=c_slabs(c_re), cd_im_neg=c_slabs(-c_im),
                d=d.reshape(1, SSM_WIDTH), w_glu=w_glu.astype(BF16))


def _post_kernel(x_ref, oa_ref, os_ref, ga_ref, gs_ref, p_ref,
                 wba_ref, wbs_ref, wout_ref, wg_ref, wu_ref, wd_ref, wpg_ref, wpp_ref,
                 n_mix_post, n_ffn_pre, n_ffn_post, n_ple_pre, n_ple_post,
                 y_ref, *, nb, tm, ff_chunk):
    rows = nb * tm
    x = x_ref[...].reshape(rows, D_MODEL)
    oa = oa_ref[...].reshape(rows, SB_WIDTH)
    os_ = os_ref[...].reshape(rows, SSM_WIDTH)
    merged = (_sigmoid(ga_ref[...].reshape(rows, D_MODEL)) * _dot(oa, wba_ref[...])
              + _sigmoid(gs_ref[...].reshape(rows, D_MODEL)) * _dot(os_, wbs_ref[...]))
    x = x + _rms(_dot(merged.astype(BF16), wout_ref[...]), n_mix_post[...])

    f = _rms(x, n_ffn_pre[...]).astype(BF16)
    ff = jnp.zeros((rows, D_MODEL), F32)
    for c in range(D_FF // ff_chunk):
        cols = slice(c * ff_chunk, (c + 1) * ff_chunk)
        g = _dot(f, wg_ref[:, cols])
        a = (g * _sigmoid(g)) * _dot(f, wu_ref[:, cols])
        ff = ff + _dot(a.astype(BF16), wd_ref[cols, :])
    x = x + _rms(ff, n_ffn_post[...])

    gate = _sigmoid(_dot(_rms(x, n_ple_pre[...]).astype(BF16), wpg_ref[...]))
    pe = gate * _dot(p_ref[...].reshape(rows, PLE_DIM).astype(BF16), wpp_ref[...])
    y_ref[...] = (x + _rms(pe, n_ple_post[...])).reshape(nb, tm, D_MODEL)


def _post(x, o_attn, o_ssm, g_attn, g_ssm, p, W, *, nb, tm, ff_chunk=256):
    bsz, L, _ = x.shape
    grid = (bsz // nb, L // tm)
    tok = lambda width: pl.BlockSpec((nb, tm, width), lambda b, i: (b, i, 0))
    vec = _const_spec((1, D_MODEL))
    return pl.pallas_call(
        functools.partial(_post_kernel, nb=nb, tm=tm, ff_chunk=ff_chunk),
        grid=grid,
        in_specs=[tok(D_MODEL), tok(SB_WIDTH), tok(SSM_WIDTH),
                  tok(D_MODEL), tok(D_MODEL), tok(PLE_DIM),
                  _const_spec((SB_WIDTH, D_MODEL)), _const_spec((SSM_WIDTH, D_MODEL)),
                  _const_spec((D_MODEL, D_MODEL)),
                  _const_spec((D_MODEL, D_FF)), _const_spec((D_MODEL, D_FF)), _const_spec((D_FF, D_MODEL)),
                  _const_spec((D_MODEL, D_MODEL)), _const_spec((PLE_DIM, D_MODEL)),
                  vec, vec, vec, vec, vec],
        out_specs=tok(D_MODEL),
        out_shape=jax.ShapeDtypeStruct((bsz, L, D_MODEL), F32),
        compiler_params=pltpu.CompilerParams(
            dimension_semantics=("parallel", "parallel"), vmem_limit_bytes=VMEM_LIMIT),
        name="post",
    )(x, o_attn, o_ssm, g_attn, g_ssm, p,
      W["w_branch_attn"], W["w_branch_ssm"], W["w_out"], W["w_ffn_gate"], W["w_ffn_up"], W["w_ffn_down"],
      W["w_ple_gate"], W["w_ple_proj"],
      W["norm_mix_post"], W["norm_ffn_pre"], W["norm_ffn_post"], W["norm_ple_pre"], W["norm_ple_post"])


def _layer(x, p, cache, s_re0, s_im0, W, *, nb, tm, bq, tc, post_nb, post_tm):
    bsz, L, _ = x.shape
    q, k, v, u, g_attn, g_ssm = _inproj(x, W["norm_mix_pre"], W["w_in"], W["w_kv_t"], nb=nb, tm=tm,
                                           kv_t=cache is None)
    if cache is None:
        o_attn = _attention(q, k, v, bq=bq)
        k, v = (jnp.transpose(a, (0, 3, 1, 2)) for a in (k, v))
    else:
        o_attn = _attention_cached(q, k, v, *cache)
    o_ssm, s_re, s_im = _ssm(u, s_re0.reshape(bsz, SSM_LANES), s_im0.reshape(bsz, SSM_LANES), W["ssm"], tc=tc)
    y = _post(x, o_attn, o_ssm, g_attn, g_ssm, p, W, nb=post_nb, tm=post_tm)
    heads = (bsz, L, SB_HEADS, SB_HEAD_DIM)
    state = (bsz, SSM_GROUPS, SSM_STATE)
    return y, k.reshape(heads), v.reshape(heads), s_re.reshape(state), s_im.reshape(state)


def kernel(x_prompt, x_sample, cache_k, cache_v, state_ssm_re, state_ssm_im, p_prompt, p_sample, norm_mix_pre, norm_mix_post, w_in, ssm_a_re, ssm_a_im, ssm_log_dt, ssm_b_re, ssm_b_im, ssm_c_re, ssm_c_im, ssm_d, w_glu, w_branch_attn, w_branch_ssm, w_out, norm_ffn_pre, norm_ffn_post, w_ffn_gate, w_ffn_up, w_ffn_down, norm_ple_pre, norm_ple_post, w_ple_gate, w_ple_proj):
    depth = w_in.shape[0]
    yp, ys = x_prompt, x_sample
    outs = [[] for _ in range(8)]
    cache_kt, cache_vt = (jnp.transpose(c, (0, 1, 3, 4, 2)) for c in (cache_k, cache_v))
    for i in range(depth):
        W = dict(
            norm_mix_pre=norm_mix_pre[i][None], norm_mix_post=norm_mix_post[i][None],
            norm_ffn_pre=norm_ffn_pre[i][None], norm_ffn_post=norm_ffn_post[i][None],
            norm_ple_pre=norm_ple_pre[i][None], norm_ple_post=norm_ple_post[i][None],
            w_in=w_in[i].astype(BF16), w_kv_t=w_in[i][:, SB_WIDTH:3 * SB_WIDTH].T.astype(BF16),
            w_branch_attn=w_branch_attn[i].astype(BF16),
            w_branch_ssm=w_branch_ssm[i].astype(BF16), w_out=w_out[i].astype(BF16),
            w_ffn_gate=w_ffn_gate[i].astype(BF16), w_ffn_up=w_ffn_up[i].astype(BF16),
            w_ffn_down=w_ffn_down[i].astype(BF16), w_ple_gate=w_ple_gate[i].astype(BF16),
            w_ple_proj=w_ple_proj[i].astype(BF16),
            ssm=_ssm_params(ssm_a_re[i], ssm_a_im[i], ssm_log_dt[i], ssm_b_re[i], ssm_b_im[i],
                            ssm_c_re[i], ssm_c_im[i], ssm_d[i], w_glu[i]))
        bp, lp = yp.shape[0], yp.shape[1]
        bs, ls = ys.shape[0], ys.shape[1]
        zero_state = jnp.zeros((bp, SSM_GROUPS, SSM_STATE), F32)
        yp, kp, vp, srp, sip = _layer(
            yp, p_prompt[i], None, zero_state, zero_state, W,
            nb=1, tm=min(512, lp), bq=min(256, lp), tc=512 // bp, post_nb=1, post_tm=min(512, lp))
        ys, kn, vn, srs, sis = _layer(
            ys, p_sample[i], (cache_kt, cache_vt, i), state_ssm_re[i], state_ssm_im[i], W,
            nb=512 // ls, tm=ls, bq=ls, tc=512 // bs, post_nb=512 // ls, post_tm=ls)
        for lst, val in zip(outs, (kp, vp, srp, sip, kn, vn, srs, sis)):
            lst.append(val)
    return (yp, ys) + tuple(jnp.stack(o) for o in outs)
```

```python
import functools
import math

import jax
import jax.numpy as jnp
from jax import lax
from jax.experimental import pallas as pl
from jax.experimental.pallas import tpu as pltpu

F32 = jnp.float32
BF16 = jnp.bfloat16

D_MODEL = 1024
PLE_DIM = 256
SB_WIDTH = 512
SB_HEAD_DIM = 64
SB_HEADS = 8
SSM_WIDTH = 512
SSM_GROUP = 16
SSM_GROUPS = 32
SSM_STATE = 64
SSM_LANES = SSM_GROUPS * SSM_STATE
D_FF = 2816
IN_WIDTH = 3 * SB_WIDTH + SSM_WIDTH + 2 * D_MODEL
RMS_EPS = 1e-6

LANES = 128
HEADS_PER_BLOCK = LANES // SB_HEAD_DIM
SSM_SLABS = SSM_WIDTH // LANES
SLAB_STATE = SSM_LANES // SSM_SLABS
VMEM_LIMIT = 56 * 1024 * 1024


def _const_spec(shape):
    nd = len(shape)
    return pl.BlockSpec(shape, lambda *_: (0,) * nd, pipeline_mode=pl.Buffered(1))


def _rms(x, gain):
    ms = jnp.mean(x * x, axis=-1, keepdims=True)
    return x * lax.rsqrt(ms + RMS_EPS) * gain


def _sigmoid(x):
    return 1.0 / (1.0 + jnp.exp(-x))


def _dot(a, b):
    return jnp.dot(a, b, preferred_element_type=F32)


def _dot_nt(a, b):
    return lax.dot_general(a, b, (((1,), (1,)), ((), ())), preferred_element_type=F32)


def _inproj_kernel(*refs, nb, tm, kv_t):
    if kv_t:
        x_ref, gain_ref, w_ref, wkvt_ref, q_ref, k_ref, v_ref, u_ref, ga_ref, gs_ref = refs
    else:
        x_ref, gain_ref, w_ref, q_ref, k_ref, v_ref, u_ref, ga_ref, gs_ref = refs
    x = x_ref[...].reshape(nb * tm, D_MODEL)
    h = _rms(x, gain_ref[...]).astype(BF16)

    def proj(lo, width):
        return _dot(h, w_ref[:, lo:lo + width])

    q_ref[...] = (proj(0, SB_WIDTH) * (SB_HEAD_DIM ** -0.5)).astype(BF16).reshape(nb, tm, SB_WIDTH)
    if kv_t:
        k_ref[...] = _dot_nt(wkvt_ref[0:SB_WIDTH, :], h).reshape(SB_HEADS, SB_HEAD_DIM, tm)
        v_ref[...] = _dot_nt(wkvt_ref[SB_WIDTH:2 * SB_WIDTH, :], h).reshape(SB_HEADS, SB_HEAD_DIM, tm)
    else:
        k_ref[...] = proj(SB_WIDTH, SB_WIDTH).reshape(nb, tm, SB_WIDTH)
        v_ref[...] = proj(2 * SB_WIDTH, SB_WIDTH).reshape(nb, tm, SB_WIDTH)
    u_ref[...] = proj(3 * SB_WIDTH, SSM_WIDTH).reshape(nb, tm, SSM_WIDTH)
    off = 3 * SB_WIDTH + SSM_WIDTH
    ga_ref[...] = proj(off, D_MODEL).reshape(nb, tm, D_MODEL)
    gs_ref[...] = proj(off + D_MODEL, D_MODEL).reshape(nb, tm, D_MODEL)


def _inproj(x, gain, w_in, w_kv_t, *, nb, tm, kv_t):
    bsz, L, _ = x.shape
    grid = (bsz // nb, L // tm)
    tok = lambda width: pl.BlockSpec((nb, tm, width), lambda b, i: (b, i, 0))
    in_specs = [tok(D_MODEL), _const_spec((1, D_MODEL)), _const_spec((D_MODEL, IN_WIDTH))]
    args = [x, gain, w_in]
    if kv_t:
        assert nb == 1
        in_specs.append(_const_spec((2 * SB_WIDTH, D_MODEL)))
        args.append(w_kv_t)
        kv_spec = pl.BlockSpec((None, SB_HEADS, SB_HEAD_DIM, tm), lambda b, i: (b, 0, 0, i))
        kv_shape = jax.ShapeDtypeStruct((bsz, SB_HEADS, SB_HEAD_DIM, L), F32)
    else:
        kv_spec = tok(SB_WIDTH)
        kv_shape = jax.ShapeDtypeStruct((bsz, L, SB_WIDTH), F32)
    return pl.pallas_call(
        functools.partial(_inproj_kernel, nb=nb, tm=tm, kv_t=kv_t),
        grid=grid,
        in_specs=in_specs,
        out_specs=[tok(SB_WIDTH), kv_spec, kv_spec, tok(SSM_WIDTH), tok(D_MODEL), tok(D_MODEL)],
        out_shape=[jax.ShapeDtypeStruct((bsz, L, SB_WIDTH), BF16), kv_shape, kv_shape,
                   jax.ShapeDtypeStruct((bsz, L, SSM_WIDTH), F32),
                   jax.ShapeDtypeStruct((bsz, L, D_MODEL), F32),
                   jax.ShapeDtypeStruct((bsz, L, D_MODEL), F32)],
        compiler_params=pltpu.CompilerParams(
            dimension_semantics=("parallel", "parallel"), vmem_limit_bytes=VMEM_LIMIT),
        name="inproj",
    )(*args)


LOG2E = 1.0 / math.log(2.0)
SKIP_LOG2 = 105.0 * LOG2E


SOFTPLUS2_LINEAR = 100.0


def _softplus2(y):
    return jnp.maximum(y, jnp.log2(1.0 + jnp.exp2(jnp.minimum(y, SOFTPLUS2_LINEAR))))


def _suffix_sums(sp, tri):
    return _dot(sp.astype(BF16), tri)


def _sb_weights(y, cum, c, visible):
    w = jnp.exp2(y - cum - c)
    if visible is not None:
        w = jnp.where(visible, w, 0.0)
    return w.astype(BF16)


def _unfinished(*cs):
    m = jnp.min(cs[0])
    for c in cs[1:]:
        m = jnp.minimum(m, jnp.min(c))
    return (m < SKIP_LOG2).astype(jnp.int32)


def _more(st):
    return (st[0] >= 0) & (st[1] > 0)


def _attn_kernel(q_ref, k_ref, v_ref, o_ref, c_ref, acc_ref, *, bq):
    i = pl.program_id(1)
    n_pairs = SB_WIDTH // LANES
    lane = lax.broadcasted_iota(jnp.int32, (bq, LANES), 1)
    row = lax.broadcasted_iota(jnp.int32, (bq, bq), 0)
    col = lax.broadcasted_iota(jnp.int32, (bq, bq), 1)
    tri = (row >= col).astype(BF16)
    diag_visible = col < row

    in_head = [(lane >= h * SB_HEAD_DIM) & (lane < (h + 1) * SB_HEAD_DIM) for h in range(HEADS_PER_BLOCK)]
    qs = []
    for p in range(n_pairs):
        qp = q_ref[:, p * LANES:(p + 1) * LANES]
        qs += [jnp.where(m, qp, jnp.zeros_like(qp)) for m in in_head]

    def visit(j, visible, first):
        s0 = pl.multiple_of(j * bq, bq)
        kts, vts = [], []
        for p in range(n_pairs):
            heads = slice(p * HEADS_PER_BLOCK, (p + 1) * HEADS_PER_BLOCK)
            kts.append(k_ref[heads, :, pl.ds(s0, bq)].reshape(LANES, bq).astype(BF16))
            vts.append(v_ref[heads, :, pl.ds(s0, bq)].reshape(LANES, bq).astype(BF16))
        ys = [_dot(qs[h], kts[h // HEADS_PER_BLOCK]) * LOG2E for h in range(SB_HEADS)]
        sps = [_softplus2(y) if visible is None else jnp.where(visible, _softplus2(y), 0.0) for y in ys]
        cums = [_suffix_sums(sp, tri) for sp in sps]
        cs_old = [0.0 if first else c_ref[h] for h in range(SB_HEADS)]
        ws = [_sb_weights(ys[h], cums[h], cs_old[h], visible) for h in range(SB_HEADS)]
        accs = [_dot_nt(ws[h], vts[h // HEADS_PER_BLOCK]) for h in range(SB_HEADS)]
        cs = [cums[h][:, 0:1] + cs_old[h] for h in range(SB_HEADS)]
        for h in range(SB_HEADS):
            acc_ref[h] = accs[h] if first else acc_ref[h] + accs[h]
            c_ref[h] = cs[h]
        return _unfinished(*cs)

    go = visit(i, diag_visible, True)
    lax.while_loop(_more, lambda st: (st[0] - 1, visit(st[0], None, False)), (i - 1, go))
    for p in range(n_pairs):
        o_ref[:, p * LANES:(p + 1) * LANES] = jnp.where(
            in_head[0], acc_ref[p * HEADS_PER_BLOCK], acc_ref[p * HEADS_PER_BLOCK + 1]).astype(o_ref.dtype)


def _attention(q, k_t, v_t, *, bq):
    bsz, L, _ = q.shape
    qspec = pl.BlockSpec((None, bq, SB_WIDTH), lambda b, i: (b, i, 0))
    kvspec = pl.BlockSpec((None, SB_HEADS, SB_HEAD_DIM, L), lambda b, i: (b, 0, 0, 0))
    return pl.pallas_call(
        functools.partial(_attn_kernel, bq=bq),
        grid=(bsz, L // bq),
        in_specs=[qspec, kvspec, kvspec],
        out_specs=qspec,
        out_shape=jax.ShapeDtypeStruct((bsz, L, SB_WIDTH), BF16),
        scratch_shapes=[pltpu.VMEM((SB_HEADS, bq, 1), F32), pltpu.VMEM((SB_HEADS, bq, LANES), F32)],
        compiler_params=pltpu.CompilerParams(
            dimension_semantics=("parallel", "parallel"), vmem_limit_bytes=VMEM_LIMIT),
        name="sb_attention",
    )(q, k_t, v_t)


def _attn_cached_kernel(q_ref, k_ref, v_ref, kl_ref, vl_ref, kc_hbm, vc_hbm, o_ref, kbuf, vbuf, sem,
                        *, layer, lq, bk, n_past):
    b = pl.program_id(0)
    hd = SB_HEAD_DIM
    q = q_ref[...]
    kn = k_ref[...].astype(BF16)
    vn = v_ref[...].astype(BF16)
    head = lambda x, h: x[:, h * hd:(h + 1) * hd]
    qs = [head(q, h) for h in range(SB_HEADS)]
    rows = SB_HEADS * lq

    r = lax.broadcasted_iota(jnp.int32, (rows, lq), 0) % lq
    s = lax.broadcasted_iota(jnp.int32, (rows, lq), 1)
    visible = s < r
    tr = lax.broadcasted_iota(jnp.int32, (lq, lq), 0)
    tc = lax.broadcasted_iota(jnp.int32, (lq, lq), 1)
    yd = jnp.concatenate([_dot_nt(qs[h], head(kn, h)) for h in range(SB_HEADS)], axis=0) * LOG2E
    cumd = _suffix_sums(jnp.where(visible, _softplus2(yd), 0.0), (tr >= tc).astype(BF16))
    wd = _sb_weights(yd, cumd, 0.0, visible)
    accs = [_dot(wd[h * lq:(h + 1) * lq], head(vn, h)) for h in range(SB_HEADS)]
    c = cumd[:, 0:1]

    pr = lax.broadcasted_iota(jnp.int32, (bk, bk), 0)
    pc = lax.broadcasted_iota(jnp.int32, (bk, bk), 1)
    tri_p = (pr >= pc).astype(BF16)

    def cache_block(kref, vref, c, accs):
        y = jnp.concatenate([_dot(qs[h], kref[h].astype(BF16)) for h in range(SB_HEADS)], axis=0) * LOG2E
        cum = _suffix_sums(_softplus2(y), tri_p)
        w = _sb_weights(y, cum, c, None)
        accs = [accs[h] + _dot_nt(w[h * lq:(h + 1) * lq], vref[h].astype(BF16)) for h in range(SB_HEADS)]
        return c + cum[:, 0:1], accs

    c, accs = cache_block(kl_ref, vl_ref, c, accs)

    def fetch(j):
        s0 = pl.multiple_of(j * bk, bk)
        return (pltpu.make_async_copy(kc_hbm.at[layer, b, :, :, pl.ds(s0, bk)], kbuf, sem.at[0]),
                pltpu.make_async_copy(vc_hbm.at[layer, b, :, :, pl.ds(s0, bk)], vbuf, sem.at[1]))

    def visit(st):
        j, c, accs = st[0], st[2], list(st[3:])
        copies = fetch(j)
        for cp in copies:
            cp.start()
        for cp in copies:
            cp.wait()
        c, accs = cache_block(kbuf, vbuf, c, accs)
        return (j - 1, _unfinished(c), c) + tuple(accs)

    st = lax.while_loop(_more, visit, (n_past - 2, _unfinished(c), c) + tuple(accs))
    o_ref[...] = jnp.concatenate(st[3:], axis=1).astype(o_ref.dtype)


def _attention_cached(q, k, v, cache_kt, cache_vt, layer, *, bk=256):
    bsz, lq, _ = q.shape
    past = cache_kt.shape[4]
    n_past = past // bk
    tok = pl.BlockSpec((None, lq, SB_WIDTH), lambda b: (b, 0, 0))
    last = pl.BlockSpec((None, None, SB_HEADS, SB_HEAD_DIM, bk), lambda b: (layer, b, 0, 0, n_past - 1))
    hbm = pl.BlockSpec(memory_space=pl.ANY)
    return pl.pallas_call(
        functools.partial(_attn_cached_kernel, layer=layer, lq=lq, bk=bk, n_past=n_past),
        grid=(bsz,),
        in_specs=[tok, tok, tok, last, last, hbm, hbm],
        out_specs=tok,
        out_shape=jax.ShapeDtypeStruct((bsz, lq, SB_WIDTH), BF16),
        scratch_shapes=[pltpu.VMEM((SB_HEADS, SB_HEAD_DIM, bk), F32), pltpu.VMEM((SB_HEADS, SB_HEAD_DIM, bk), F32),
                        pltpu.SemaphoreType.DMA((2,))],
        compiler_params=pltpu.CompilerParams(
            dimension_semantics=("parallel",), vmem_limit_bytes=VMEM_LIMIT),
        name="sb_attention_cached",
    )(q, k, v, cache_kt, cache_vt, cache_kt, cache_vt)


def _gelu_tanh(x):
    return 0.5 * x * (1.0 + jnp.tanh(math.sqrt(2.0 / math.pi) * (x + 0.044715 * (x * x * x))))


SSM_ROW_PARTS = 4


def _ssm_kernel(u_ref, s0re_ref, s0im_ref, lbre_ref, lbim_ref, bre_ref, bim_ref, cre_ref, cimn_ref,
                d_ref, wglu_ref, o_ref, sfre_ref, sfim_ref, sre, sim, st_re, st_im, tmaj, ybuf, *, bsz, tc):
    step = pl.program_id(0)
    rows = tc * bsz
    part_rows = rows // SSM_ROW_PARTS
    part_steps = tc // SSM_ROW_PARTS

    @pl.when(step == 0)
    def _():
        st_re[...] = s0re_ref[...]
        st_im[...] = s0im_ref[...]

    for b in range(bsz):
        for m in range(SSM_SLABS):
            tmaj[m, pl.ds(b, tc, stride=bsz), :] = u_ref[b, :, m * LANES:(m + 1) * LANES]

    def part(q):
        return slice(q * part_rows, (q + 1) * part_rows)

    def slab(m):
        return slice(m * SLAB_STATE, (m + 1) * SLAB_STATE)

    def b_proj(m, q):
        um = tmaj[m, part(q), :].astype(BF16)
        sre[part(q), slab(m)] = _dot(um, bre_ref[m])
        sim[part(q), slab(m)] = _dot(um, bim_ref[m])

    def recurrence(m, q):
        lr = jnp.broadcast_to(lbre_ref[:, slab(m)], (8, SLAB_STATE))
        li = jnp.broadcast_to(lbim_ref[:, slab(m)], (8, SLAB_STATE))
        for sb in range(bsz // 8):
            subl = slice(sb * 8, (sb + 1) * 8)
            sr, si = st_re[subl, slab(m)], st_im[subl, slab(m)]
            for t in range(q * part_steps, (q + 1) * part_steps):
                r = slice(t * bsz + sb * 8, t * bsz + sb * 8 + 8)
                sr, si = (lr * sr - li * si + sre[r, slab(m)], lr * si + li * sr + sim[r, slab(m)])
                sre[r, slab(m)] = sr
                sim[r, slab(m)] = si
            st_re[subl, slab(m)] = sr
            st_im[subl, slab(m)] = si

    def c_proj(m, q):
        ybuf[m, part(q), :] = (_dot(sre[part(q), slab(m)].astype(BF16), cre_ref[m])
                               + _dot(sim[part(q), slab(m)].astype(BF16), cimn_ref[m]))

    for q in range(SSM_ROW_PARTS):
        b_proj(0, q)
    for m in range(SSM_SLABS):
        for q in range(SSM_ROW_PARTS):
            recurrence(m, q)
            if m + 1 < SSM_SLABS:
                b_proj(m + 1, q)
            c_proj(m, q)

    u = jnp.concatenate([tmaj[m] for m in range(SSM_SLABS)], axis=1)
    y = jnp.concatenate([ybuf[m] for m in range(SSM_SLABS)], axis=1) + d_ref[...] * u
    y = _gelu_tanh(y)
    o = y * _sigmoid(_dot(y.astype(BF16), wglu_ref[...]))
    for m in range(SSM_SLABS):
        tmaj[m] = o[:, m * LANES:(m + 1) * LANES]
    for b in range(bsz):
        for m in range(SSM_SLABS):
            o_ref[b, :, m * LANES:(m + 1) * LANES] = tmaj[m, pl.ds(b, tc, stride=bsz), :].astype(o_ref.dtype)

    @pl.when(step == pl.num_programs(0) - 1)
    def _():
        sfre_ref[...] = st_re[...]
        sfim_ref[...] = st_im[...]


def _ssm(u, s0_re, s0_im, prm, *, tc):
    bsz, L, _ = u.shape
    blk = tc * bsz
    row_spec = pl.BlockSpec((bsz, tc, SSM_WIDTH), lambda s: (0, s, 0))
    st_shape = (bsz, SSM_LANES)
    return pl.pallas_call(
        functools.partial(_ssm_kernel, bsz=bsz, tc=tc),
        grid=(L // tc,),
        in_specs=[row_spec, _const_spec(st_shape), _const_spec(st_shape),
                  _const_spec((1, SSM_LANES)), _const_spec((1, SSM_LANES)),
                  _const_spec((SSM_SLABS, LANES, SLAB_STATE)), _const_spec((SSM_SLABS, LANES, SLAB_STATE)),
                  _const_spec((SSM_SLABS, SLAB_STATE, LANES)), _const_spec((SSM_SLABS, SLAB_STATE, LANES)),
                  _const_spec((1, SSM_WIDTH)), _const_spec((SSM_WIDTH, SSM_WIDTH))],
        out_specs=[row_spec, _const_spec(st_shape), _const_spec(st_shape)],
        out_shape=[jax.ShapeDtypeStruct((bsz, L, SSM_WIDTH), BF16),
                   jax.ShapeDtypeStruct(st_shape, F32), jax.ShapeDtypeStruct(st_shape, F32)],
        scratch_shapes=[pltpu.VMEM((blk, SSM_LANES), F32), pltpu.VMEM((blk, SSM_LANES), F32),
                        pltpu.VMEM(st_shape, F32), pltpu.VMEM(st_shape, F32),
                        pltpu.VMEM((SSM_SLABS, blk, LANES), F32),
                        pltpu.VMEM((SSM_SLABS, blk, LANES), F32)],
        compiler_params=pltpu.CompilerParams(
            dimension_semantics=("arbitrary",), vmem_limit_bytes=VMEM_LIMIT),
        name="s5_scan",
    )(u, s0_re, s0_im, prm["lb_re"], prm["lb_im"], prm["bd_re"], prm["bd_im"],
      prm["cd_re"], prm["cd_im_neg"], prm["d"], prm["w_glu"])


def _ssm_params(a_re, a_im, log_dt, b_re, b_im, c_re, c_im, d, w_glu):
    dt = jnp.exp(log_dt)[:, None]
    mag = jnp.exp(a_re * dt)
    lb_re = mag * jnp.cos(a_im * dt)
    lb_im = mag * jnp.sin(a_im * dt)
    den = a_re * a_re + a_im * a_im
    nr, ni = lb_re - 1.0, lb_im
    f_re = (nr * a_re + ni * a_im) / den
    f_im = (ni * a_re - nr * a_im) / den
    bb_re = f_re[:, :, None] * b_re - f_im[:, :, None] * b_im
    bb_im = f_re[:, :, None] * b_im + f_im[:, :, None] * b_re
    gps = SSM_GROUPS // SSM_SLABS
    eye = jnp.eye(gps, dtype=F32)

    def b_slabs(bb):
        t = bb.transpose(0, 2, 1).reshape(SSM_SLABS, gps, SSM_GROUP, SSM_STATE)
        return jnp.einsum("mgcp,gh->mgchp", t, eye).reshape(SSM_SLABS, LANES, SLAB_STATE).astype(BF16)

    def c_slabs(c):
        t = c.transpose(0, 2, 1).reshape(SSM_SLABS, gps, SSM_STATE, SSM_GROUP)
        return jnp.einsum("mgpc,gh->mgphc", t, eye).reshape(SSM_SLABS, SLAB_STATE, LANES).astype(BF16)

    return dict(lb_re=lb_re.reshape(1, SSM_LANES), lb_im=lb_im.reshape(1, SSM_LANES),
                bd_re=b_slabs(bb_re), bd_im=b_slabs(bb_im),
                cd_re=c_slabs(c_re), cd_im_neg=c_slabs(-c_im),
                d=d.reshape(1, SSM_WIDTH), w_glu=w_glu.astype(BF16))


def _post_kernel(x_ref, oa_ref, os_ref, ga_ref, gs_ref, p_ref,
                 wba_ref, wbs_ref, wout_ref, wg_ref, wu_ref, wd_ref, wpg_ref, wpp_ref,
                 n_mix_post, n_ffn_pre, n_ffn_post, n_ple_pre, n_ple_post,
                 y_ref, *, nb, tm, ff_chunk):
    rows = nb * tm
    x = x_ref[...].reshape(rows, D_MODEL)
    oa = oa_ref[...].reshape(rows, SB_WIDTH)
    os_ = os_ref[...].reshape(rows, SSM_WIDTH)
    merged = (_sigmoid(ga_ref[...].reshape(rows, D_MODEL)) * _dot(oa, wba_ref[...])
              + _sigmoid(gs_ref[...].reshape(rows, D_MODEL)) * _dot(os_, wbs_ref[...]))
    x = x + _rms(_dot(merged.astype(BF16), wout_ref[...]), n_mix_post[...])

    f = _rms(x, n_ffn_pre[...]).astype(BF16)
    ff = jnp.zeros((rows, D_MODEL), F32)
    for c in range(D_FF // ff_chunk):
        cols = slice(c * ff_chunk, (c + 1) * ff_chunk)
        g = _dot(f, wg_ref[:, cols])
        a = (g * _sigmoid(g)) * _dot(f, wu_ref[:, cols])
        ff = ff + _dot(a.astype(BF16), wd_ref[cols, :])
    x = x + _rms(ff, n_ffn_post[...])

    gate = _sigmoid(_dot(_rms(x, n_ple_pre[...]).astype(BF16), wpg_ref[...]))
    pe = gate * _dot(p_ref[...].reshape(rows, PLE_DIM).astype(BF16), wpp_ref[...])
    y_ref[...] = (x + _rms(pe, n_ple_post[...])).reshape(nb, tm, D_MODEL)


def _post(x, o_attn, o_ssm, g_attn, g_ssm, p, W, *, nb, tm, ff_chunk=256):
    bsz, L, _ = x.shape
    grid = (bsz // nb, L // tm)
    tok = lambda width: pl.BlockSpec((nb, tm, width), lambda b, i: (b, i, 0))
    vec = _const_spec((1, D_MODEL))
    return pl.pallas_call(
        functools.partial(_post_kernel, nb=nb, tm=tm, ff_chunk=ff_chunk),
        grid=grid,
        in_specs=[tok(D_MODEL), tok(SB_WIDTH), tok(SSM_WIDTH),
                  tok(D_MODEL), tok(D_MODEL), tok(PLE_DIM),
                  _const_spec((SB_WIDTH, D_MODEL)), _const_spec((SSM_WIDTH, D_MODEL)),
                  _const_spec((D_MODEL, D_MODEL)),
                  _const_spec((D_MODEL, D_FF)), _const_spec((D_MODEL, D_FF)), _const_spec((D_FF, D_MODEL)),
                  _const_spec((D_MODEL, D_MODEL)), _const_spec((PLE_DIM, D_MODEL)),
                  vec, vec, vec, vec, vec],
        out_specs=tok(D_MODEL),
        out_shape=jax.ShapeDtypeStruct((bsz, L, D_MODEL), F32),
        compiler_params=pltpu.CompilerParams(
            dimension_semantics=("parallel", "parallel"), vmem_limit_bytes=VMEM_LIMIT),
        name="post",
    )(x, o_attn, o_ssm, g_attn, g_ssm, p,
      W["w_branch_attn"], W["w_branch_ssm"], W["w_out"], W["w_ffn_gate"], W["w_ffn_up"], W["w_ffn_down"],
      W["w_ple_gate"], W["w_ple_proj"],
      W["norm_mix_post"], W["norm_ffn_pre"], W["norm_ffn_post"], W["norm_ple_pre"], W["norm_ple_post"])


def _layer(x, p, cache, s_re0, s_im0, W, *, nb, tm, bq, tc, post_nb, post_tm):
    bsz, L, _ = x.shape
    q, k, v, u, g_attn, g_ssm = _inproj(x, W["norm_mix_pre"], W["w_in"], W["w_kv_t"], nb=nb, tm=tm,
                                           kv_t=cache is None)
    if cache is None:
        o_attn = _attention(q, k, v, bq=bq)
        k, v = (jnp.transpose(a, (0, 3, 1, 2)) for a in (k, v))
    else:
        o_attn = _attention_cached(q, k, v, *cache)
    o_ssm, s_re, s_im = _ssm(u, s_re0.reshape(bsz, SSM_LANES), s_im0.reshape(bsz, SSM_LANES), W["ssm"], tc=tc)
    y = _post(x, o_attn, o_ssm, g_attn, g_ssm, p, W, nb=post_nb, tm=post_tm)
    heads = (bsz, L, SB_HEADS, SB_HEAD_DIM)
    state = (bsz, SSM_GROUPS, SSM_STATE)
    return y, k.reshape(heads), v.reshape(heads), s_re.reshape(state), s_im.reshape(state)


def kernel(x_prompt, x_sample, cache_k, cache_v, state_ssm_re, state_ssm_im, p_prompt, p_sample, norm_mix_pre, norm_mix_post, w_in, ssm_a_re, ssm_a_im, ssm_log_dt, ssm_b_re, ssm_b_im, ssm_c_re, ssm_c_im, ssm_d, w_glu, w_branch_attn, w_branch_ssm, w_out, norm_ffn_pre, norm_ffn_post, w_ffn_gate, w_ffn_up, w_ffn_down, norm_ple_pre, norm_ple_post, w_ple_gate, w_ple_proj):
    depth = w_in.shape[0]
    yp, ys = x_prompt, x_sample
    outs = [[] for _ in range(8)]
    cache_kt, cache_vt = (jnp.transpose(c, (0, 1, 3, 4, 2)) for c in (cache_k, cache_v))
    for i in range(depth):
        W = dict(
            norm_mix_pre=norm_mix_pre[i][None], norm_mix_post=norm_mix_post[i][None],
            norm_ffn_pre=norm_ffn_pre[i][None], norm_ffn_post=norm_ffn_post[i][None],
            norm_ple_pre=norm_ple_pre[i][None], norm_ple_post=norm_ple_post[i][None],
            w_in=w_in[i].astype(BF16), w_kv_t=w_in[i][:, SB_WIDTH:3 * SB_WIDTH].T.astype(BF16),
            w_branch_attn=w_branch_attn[i].astype(BF16),
            w_branch_ssm=w_branch_ssm[i].astype(BF16), w_out=w_out[i].astype(BF16),
            w_ffn_gate=w_ffn_gate[i].astype(BF16), w_ffn_up=w_ffn_up[i].astype(BF16),
            w_ffn_down=w_ffn_down[i].astype(BF16), w_ple_gate=w_ple_gate[i].astype(BF16),
            w_ple_proj=w_ple_proj[i].astype(BF16),
            ssm=_ssm_params(ssm_a_re[i], ssm_a_im[i], ssm_log_dt[i], ssm_b_re[i], ssm_b_im[i],
                            ssm_c_re[i], ssm_c_im[i], ssm_d[i], w_glu[i]))
        bp, lp = yp.shape[0], yp.shape[1]
        bs, ls = ys.shape[0], ys.shape[1]
        zero_state = jnp.zeros((bp, SSM_GROUPS, SSM_STATE), F32)
        yp, kp, vp, srp, sip = _layer(
            yp, p_prompt[i], None, zero_state, zero_state, W,
            nb=1, tm=min(512, lp), bq=min(256, lp), tc=512 // bp, post_nb=1, post_tm=min(512, lp))
        ys, kn, vn, srs, sis = _layer(
            ys, p_sample[i], (cache_kt, cache_vt, i), state_ssm_re[i], state_ssm_im[i], W,
            nb=512 // ls, tm=ls, bq=ls, tc=512 // bs, post_nb=512 // ls, post_tm=ls)
        for lst, val in zip(outs, (kp, vp, srp, sip, kn, vn, srs, sis)):
            lst.append(val)
    return (yp, ys) + tuple(jnp.stack(o) for o in outs)
```

```python
import functools
import math

import jax
import jax.numpy as jnp
from jax import lax
from jax.experimental import pallas as pl
from jax.experimental.pallas import tpu as pltpu

F32 = jnp.float32
BF16 = jnp.bfloat16

D_MODEL = 1024
PLE_DIM = 256
SB_WIDTH = 512
SB_HEAD_DIM = 64
SB_HEADS = 8
SSM_WIDTH = 512
SSM_GROUP = 16
SSM_GROUPS = 32
SSM_STATE = 64
SSM_LANES = SSM_GROUPS * SSM_STATE
D_FF = 2816
IN_WIDTH = 3 * SB_WIDTH + SSM_WIDTH + 2 * D_MODEL
RMS_EPS = 1e-6

LANES = 128
HEADS_PER_BLOCK = LANES // SB_HEAD_DIM
SSM_SLABS = SSM_WIDTH // LANES
SLAB_STATE = SSM_LANES // SSM_SLABS
VMEM_LIMIT = 56 * 1024 * 1024


def _const_spec(shape):
    nd = len(shape)
    return pl.BlockSpec(shape, lambda *_: (0,) * nd, pipeline_mode=pl.Buffered(1))


def _rms(x, gain):
    ms = jnp.mean(x * x, axis=-1, keepdims=True)
    return x * lax.rsqrt(ms + RMS_EPS) * gain


def _sigmoid(x):
    return 0.5 * jnp.tanh(0.5 * x) + 0.5


def _dot(a, b):
    return jnp.dot(a, b, preferred_element_type=F32)


def _dot_nt(a, b):
    return lax.dot_general(a, b, (((1,), (1,)), ((), ())), preferred_element_type=F32)


def _inproj_kernel(*refs, nb, tm, kv_t):
    if kv_t:
        x_ref, gain_ref, w_ref, wkvt_ref, q_ref, k_ref, v_ref, u_ref, ga_ref, gs_ref = refs
    else:
        x_ref, gain_ref, w_ref, q_ref, k_ref, v_ref, u_ref, ga_ref, gs_ref = refs
    x = x_ref[...].reshape(nb * tm, D_MODEL)
    h = _rms(x, gain_ref[...]).astype(BF16)

    def proj(lo, width):
        return _dot(h, w_ref[:, lo:lo + width])

    q_ref[...] = (proj(0, SB_WIDTH) * (SB_HEAD_DIM ** -0.5)).astype(BF16).reshape(nb, tm, SB_WIDTH)
    if kv_t:
        k_ref[...] = _dot_nt(wkvt_ref[0:SB_WIDTH, :], h).reshape(SB_HEADS, SB_HEAD_DIM, tm)
        v_ref[...] = _dot_nt(wkvt_ref[SB_WIDTH:2 * SB_WIDTH, :], h).reshape(SB_HEADS, SB_HEAD_DIM, tm)
    else:
        k_ref[...] = proj(SB_WIDTH, SB_WIDTH).reshape(nb, tm, SB_WIDTH)
        v_ref[...] = proj(2 * SB_WIDTH, SB_WIDTH).reshape(nb, tm, SB_WIDTH)
    u_ref[...] = proj(3 * SB_WIDTH, SSM_WIDTH).reshape(nb, tm, SSM_WIDTH)
    off = 3 * SB_WIDTH + SSM_WIDTH
    ga_ref[...] = proj(off, D_MODEL).reshape(nb, tm, D_MODEL)
    gs_ref[...] = proj(off + D_MODEL, D_MODEL).reshape(nb, tm, D_MODEL)


def _inproj(x, gain, w_in, w_kv_t, *, nb, tm, kv_t):
    bsz, L, _ = x.shape
    grid = (bsz // nb, L // tm)
    tok = lambda width: pl.BlockSpec((nb, tm, width), lambda b, i: (b, i, 0))
    in_specs = [tok(D_MODEL), _const_spec((1, D_MODEL)), _const_spec((D_MODEL, IN_WIDTH))]
    args = [x, gain, w_in]
    if kv_t:
        assert nb == 1
        in_specs.append(_const_spec((2 * SB_WIDTH, D_MODEL)))
        args.append(w_kv_t)
        kv_spec = pl.BlockSpec((None, SB_HEADS, SB_HEAD_DIM, tm), lambda b, i: (b, 0, 0, i))
        kv_shape = jax.ShapeDtypeStruct((bsz, SB_HEADS, SB_HEAD_DIM, L), F32)
    else:
        kv_spec = tok(SB_WIDTH)
        kv_shape = jax.ShapeDtypeStruct((bsz, L, SB_WIDTH), F32)
    return pl.pallas_call(
        functools.partial(_inproj_kernel, nb=nb, tm=tm, kv_t=kv_t),
        grid=grid,
        in_specs=in_specs,
        out_specs=[tok(SB_WIDTH), kv_spec, kv_spec, tok(SSM_WIDTH), tok(D_MODEL), tok(D_MODEL)],
        out_shape=[jax.ShapeDtypeStruct((bsz, L, SB_WIDTH), BF16), kv_shape, kv_shape,
                   jax.ShapeDtypeStruct((bsz, L, SSM_WIDTH), F32),
                   jax.ShapeDtypeStruct((bsz, L, D_MODEL), F32),
                   jax.ShapeDtypeStruct((bsz, L, D_MODEL), F32)],
        compiler_params=pltpu.CompilerParams(
            dimension_semantics=("parallel", "parallel"), vmem_limit_bytes=VMEM_LIMIT),
        name="inproj",
    )(*args)


LOG2E = 1.0 / math.log(2.0)
SKIP_LOG2 = 105.0 * LOG2E


SOFTPLUS2_LINEAR = 100.0


def _softplus2(y):
    return jnp.maximum(y, jnp.log2(1.0 + jnp.exp2(jnp.minimum(y, SOFTPLUS2_LINEAR))))


def _suffix_sums(sp, tri):
    return _dot(sp.astype(BF16), tri)


def _sb_weights(y, cum, c, visible):
    w = jnp.exp2(y - cum - c)
    if visible is not None:
        w = jnp.where(visible, w, 0.0)
    return w.astype(BF16)


def _unfinished(*cs):
    m = jnp.min(cs[0])
    for c in cs[1:]:
        m = jnp.minimum(m, jnp.min(c))
    return (m < SKIP_LOG2).astype(jnp.int32)


def _more(st):
    return (st[0] >= 0) & (st[1] > 0)


def _attn_kernel(q_ref, k_ref, v_ref, o_ref, c_ref, acc_ref, *, bq):
    i = pl.program_id(1)
    n_pairs = SB_WIDTH // LANES
    lane = lax.broadcasted_iota(jnp.int32, (bq, LANES), 1)
    row = lax.broadcasted_iota(jnp.int32, (bq, bq), 0)
    col = lax.broadcasted_iota(jnp.int32, (bq, bq), 1)
    tri = (row >= col).astype(BF16)
    diag_visible = col < row

    in_head = [(lane >= h * SB_HEAD_DIM) & (lane < (h + 1) * SB_HEAD_DIM) for h in range(HEADS_PER_BLOCK)]
    qs = []
    for p in range(n_pairs):
        qp = q_ref[:, p * LANES:(p + 1) * LANES]
        qs += [jnp.where(m, qp, jnp.zeros_like(qp)) for m in in_head]

    half = bq // 2
    top, bottom, whole = slice(0, half), slice(half, bq), slice(0, bq)

    def visit(parts, first):
        chains = []
        for j, rows, cols, masked in parts:
            s0 = pl.multiple_of(j * bq, bq) + cols.start
            width = cols.stop - cols.start
            for p in range(n_pairs):
                heads = slice(p * HEADS_PER_BLOCK, (p + 1) * HEADS_PER_BLOCK)
                kt = k_ref[heads, :, pl.ds(s0, width)].reshape(LANES, width).astype(BF16)
                vt = v_ref[heads, :, pl.ds(s0, width)].reshape(LANES, width).astype(BF16)
                chains += [(h, rows, kt, vt, tri[:width, :width], diag_visible[rows, cols] if masked else None)
                           for h in range(heads.start, heads.stop)]
        ys = [_dot(qs[h][rows], kt) * LOG2E for h, rows, kt, _, _, _ in chains]
        sps = [_softplus2(y) if ch[5] is None else jnp.where(ch[5], _softplus2(y), 0.0) for y, ch in zip(ys, chains)]
        cums = [_suffix_sums(sp, ch[4]) for sp, ch in zip(sps, chains)]
        cs_old = [0.0 if first else c_ref[h, rows] for h, rows, *_ in chains]
        ws = [_sb_weights(y, cum, c, ch[5]) for y, cum, c, ch in zip(ys, cums, cs_old, chains)]
        accs = [_dot_nt(w, ch[3]) for w, ch in zip(ws, chains)]
        for (h, rows, *_), acc, cum, c in zip(chains, accs, cums, cs_old):
            acc_ref[h, rows] = acc if first else acc_ref[h, rows] + acc
            c_ref[h, rows] = cum[:, 0:1] + c

    visit([(i, whole, whole, True)], True)

    @pl.when(i > 0)
    def _():
        visit([(i - 1, top, whole, False)], False)

        @pl.when(_unfinished(c_ref[:, bottom]) > 0)
        def _():
            visit([(i - 1, bottom, whole, False)], False)

    def earlier(st):
        visit([(st[0], whole, whole, False)], False)
        return st[0] - 1, _unfinished(c_ref[...])

    lax.while_loop(_more, earlier, (i - 2, _unfinished(c_ref[...])))
    for p in range(n_pairs):
        o_ref[:, p * LANES:(p + 1) * LANES] = jnp.where(
            in_head[0], acc_ref[p * HEADS_PER_BLOCK], acc_ref[p * HEADS_PER_BLOCK + 1]).astype(o_ref.dtype)


def _attention(q, k_t, v_t, *, bq):
    bsz, L, _ = q.shape
    qspec = pl.BlockSpec((None, bq, SB_WIDTH), lambda b, i: (b, i, 0))
    kvspec = pl.BlockSpec((None, SB_HEADS, SB_HEAD_DIM, L), lambda b, i: (b, 0, 0, 0))
    return pl.pallas_call(
        functools.partial(_attn_kernel, bq=bq),
        grid=(bsz, L // bq),
        in_specs=[qspec, kvspec, kvspec],
        out_specs=qspec,
        out_shape=jax.ShapeDtypeStruct((bsz, L, SB_WIDTH), BF16),
        scratch_shapes=[pltpu.VMEM((SB_HEADS, bq, 1), F32), pltpu.VMEM((SB_HEADS, bq, LANES), F32)],
        compiler_params=pltpu.CompilerParams(
            dimension_semantics=("parallel", "parallel"), vmem_limit_bytes=VMEM_LIMIT),
        name="sb_attention",
    )(q, k_t, v_t)


def _attn_cached_kernel(q_ref, k_ref, v_ref, kl_ref, vl_ref, kc_hbm, vc_hbm, o_ref, kbuf, vbuf, sem,
                        *, layer, lq, bk, n_past):
    b = pl.program_id(0)
    hd = SB_HEAD_DIM
    q = q_ref[...]
    kn = k_ref[...].astype(BF16)
    vn = v_ref[...].astype(BF16)
    head = lambda x, h: x[:, h * hd:(h + 1) * hd]
    qs = [head(q, h) for h in range(SB_HEADS)]
    rows = SB_HEADS * lq

    r = lax.broadcasted_iota(jnp.int32, (rows, lq), 0) % lq
    s = lax.broadcasted_iota(jnp.int32, (rows, lq), 1)
    visible = s < r
    tr = lax.broadcasted_iota(jnp.int32, (lq, lq), 0)
    tc = lax.broadcasted_iota(jnp.int32, (lq, lq), 1)
    yd = jnp.concatenate([_dot_nt(qs[h], head(kn, h)) for h in range(SB_HEADS)], axis=0) * LOG2E
    cumd = _suffix_sums(jnp.where(visible, _softplus2(yd), 0.0), (tr >= tc).astype(BF16))
    wd = _sb_weights(yd, cumd, 0.0, visible)
    accs = [_dot(wd[h * lq:(h + 1) * lq], head(vn, h)) for h in range(SB_HEADS)]
    c = cumd[:, 0:1]

    pr = lax.broadcasted_iota(jnp.int32, (bk, bk), 0)
    pc = lax.broadcasted_iota(jnp.int32, (bk, bk), 1)
    tri_p = (pr >= pc).astype(BF16)

    def cache_block(kref, vref, c, accs):
        y = jnp.concatenate([_dot(qs[h], kref[h].astype(BF16)) for h in range(SB_HEADS)], axis=0) * LOG2E
        cum = _suffix_sums(_softplus2(y), tri_p)
        w = _sb_weights(y, cum, c, None)
        accs = [accs[h] + _dot_nt(w[h * lq:(h + 1) * lq], vref[h].astype(BF16)) for h in range(SB_HEADS)]
        return c + cum[:, 0:1], accs

    c, accs = cache_block(kl_ref, vl_ref, c, accs)

    def fetch(j):
        s0 = pl.multiple_of(j * bk, bk)
        return (pltpu.make_async_copy(kc_hbm.at[layer, b, :, :, pl.ds(s0, bk)], kbuf, sem.at[0]),
                pltpu.make_async_copy(vc_hbm.at[layer, b, :, :, pl.ds(s0, bk)], vbuf, sem.at[1]))

    def visit(st):
        j, c, accs = st[0], st[2], list(st[3:])
        copies = fetch(j)
        for cp in copies:
            cp.start()
        for cp in copies:
            cp.wait()
        c, accs = cache_block(kbuf, vbuf, c, accs)
        return (j - 1, _unfinished(c), c) + tuple(accs)

    st = lax.while_loop(_more, visit, (n_past - 2, _unfinished(c), c) + tuple(accs))
    o_ref[...] = jnp.concatenate(st[3:], axis=1).astype(o_ref.dtype)


def _attention_cached(q, k, v, cache_kt, cache_vt, layer, *, bk=256):
    bsz, lq, _ = q.shape
    past = cache_kt.shape[4]
    n_past = past // bk
    tok = pl.BlockSpec((None, lq, SB_WIDTH), lambda b: (b, 0, 0))
    last = pl.BlockSpec((None, None, SB_HEADS, SB_HEAD_DIM, bk), lambda b: (layer, b, 0, 0, n_past - 1))
    hbm = pl.BlockSpec(memory_space=pl.ANY)
    return pl.pallas_call(
        functools.partial(_attn_cached_kernel, layer=layer, lq=lq, bk=bk, n_past=n_past),
        grid=(bsz,),
        in_specs=[tok, tok, tok, last, last, hbm, hbm],
        out_specs=tok,
        out_shape=jax.ShapeDtypeStruct((bsz, lq, SB_WIDTH), BF16),
        scratch_shapes=[pltpu.VMEM((SB_HEADS, SB_HEAD_DIM, bk), F32), pltpu.VMEM((SB_HEADS, SB_HEAD_DIM, bk), F32),
                        pltpu.SemaphoreType.DMA((2,))],
        compiler_params=pltpu.CompilerParams(
            dimension_semantics=("parallel",), vmem_limit_bytes=VMEM_LIMIT),
        name="sb_attention_cached",
    )(q, k, v, cache_kt, cache_vt, cache_kt, cache_vt)


def _gelu_tanh(x):
    return 0.5 * x * (1.0 + jnp.tanh(math.sqrt(2.0 / math.pi) * (x + 0.044715 * (x * x * x))))


SSM_ROW_PARTS = 4


def _ssm_kernel(u_ref, s0re_ref, s0im_ref, lbre_ref, lbim_ref, bre_ref, bim_ref, cre_ref, cimn_ref,
                d_ref, wglu_ref, o_ref, sfre_ref, sfim_ref, sre, sim, st_re, st_im, tmaj, ybuf, *, bsz, tc):
    step = pl.program_id(0)
    rows = tc * bsz
    part_rows = rows // SSM_ROW_PARTS
    part_steps = tc // SSM_ROW_PARTS

    @pl.when(step == 0)
    def _():
        st_re[...] = s0re_ref[...]
        st_im[...] = s0im_ref[...]

    for b in range(bsz):
        for m in range(SSM_SLABS):
            tmaj[m, pl.ds(b, tc, stride=bsz), :] = u_ref[b, :, m * LANES:(m + 1) * LANES]

    def part(q):
        return slice(q * part_rows, (q + 1) * part_rows)

    def slab(m):
        return slice(m * SLAB_STATE, (m + 1) * SLAB_STATE)

    def b_proj(m, q):
        um = tmaj[m, part(q), :].astype(BF16)
        sre[part(q), slab(m)] = _dot(um, bre_ref[m])
        sim[part(q), slab(m)] = _dot(um, bim_ref[m])

    def recurrence(m, q):
        lr = jnp.broadcast_to(lbre_ref[:, slab(m)], (8, SLAB_STATE))
        li = jnp.broadcast_to(lbim_ref[:, slab(m)], (8, SLAB_STATE))
        for sb in range(bsz // 8):
            subl = slice(sb * 8, (sb + 1) * 8)
            sr, si = st_re[subl, slab(m)], st_im[subl, slab(m)]
            for t in range(q * part_steps, (q + 1) * part_steps):
                r = slice(t * bsz + sb * 8, t * bsz + sb * 8 + 8)
                sr, si = (lr * sr - li * si + sre[r, slab(m)], lr * si + li * sr + sim[r, slab(m)])
                sre[r, slab(m)] = sr
                sim[r, slab(m)] = si
            st_re[subl, slab(m)] = sr
            st_im[subl, slab(m)] = si

    def c_proj(m, q):
        ybuf[m, part(q), :] = (_dot(sre[part(q), slab(m)].astype(BF16), cre_ref[m])
                               + _dot(sim[part(q), slab(m)].astype(BF16), cimn_ref[m]))

    for q in range(SSM_ROW_PARTS):
        b_proj(0, q)
    for m in range(SSM_SLABS):
        for q in range(SSM_ROW_PARTS):
            recurrence(m, q)
            if m + 1 < SSM_SLABS:
                b_proj(m + 1, q)
            c_proj(m, q)

    u = jnp.concatenate([tmaj[m] for m in range(SSM_SLABS)], axis=1)
    y = jnp.concatenate([ybuf[m] for m in range(SSM_SLABS)], axis=1) + d_ref[...] * u
    y = _gelu_tanh(y)
    o = y * _sigmoid(_dot(y.astype(BF16), wglu_ref[...]))
    for m in range(SSM_SLABS):
        tmaj[m] = o[:, m * LANES:(m + 1) * LANES]
    for b in range(bsz):
        for m in range(SSM_SLABS):
            o_ref[b, :, m * LANES:(m + 1) * LANES] = tmaj[m, pl.ds(b, tc, stride=bsz), :].astype(o_ref.dtype)

    @pl.when(step == pl.num_programs(0) - 1)
    def _():
        sfre_ref[...] = st_re[...]
        sfim_ref[...] = st_im[...]


def _ssm(u, s0_re, s0_im, prm, *, tc):
    bsz, L, _ = u.shape
    blk = tc * bsz
    row_spec = pl.BlockSpec((bsz, tc, SSM_WIDTH), lambda s: (0, s, 0))
    st_shape = (bsz, SSM_LANES)
    return pl.pallas_call(
        functools.partial(_ssm_kernel, bsz=bsz, tc=tc),
        grid=(L // tc,),
        in_specs=[row_spec, _const_spec(st_shape), _const_spec(st_shape),
                  _const_spec((1, SSM_LANES)), _const_spec((1, SSM_LANES)),
                  _const_spec((SSM_SLABS, LANES, SLAB_STATE)), _const_spec((SSM_SLABS, LANES, SLAB_STATE)),
                  _const_spec((SSM_SLABS, SLAB_STATE, LANES)), _const_spec((SSM_SLABS, SLAB_STATE, LANES)),
                  _const_spec((1, SSM_WIDTH)), _const_spec((SSM_WIDTH, SSM_WIDTH))],
        out_specs=[row_spec, _const_spec(st_shape), _const_spec(st_shape)],
        out_shape=[jax.ShapeDtypeStruct((bsz, L, SSM_WIDTH), BF16),
                   jax.ShapeDtypeStruct(st_shape, F32), jax.ShapeDtypeStruct(st_shape, F32)],
        scratch_shapes=[pltpu.VMEM((blk, SSM_LANES), F32), pltpu.VMEM((blk, SSM_LANES), F32),
                        pltpu.VMEM(st_shape, F32), pltpu.VMEM(st_shape, F32),
                        pltpu.VMEM((SSM_SLABS, blk, LANES), F32),
                        pltpu.VMEM((SSM_SLABS, blk, LANES), F32)],
        compiler_params=pltpu.CompilerParams(
            dimension_semantics=("arbitrary",), vmem_limit_bytes=VMEM_LIMIT),
        name="s5_scan",
    )(u, s0_re, s0_im, prm["lb_re"], prm["lb_im"], prm["bd_re"], prm["bd_im"],
      prm["cd_re"], prm["cd_im_neg"], prm["d"], prm["w_glu"])


def _ssm_params(a_re, a_im, log_dt, b_re, b_im, c_re, c_im, d, w_glu):
    dt = jnp.exp(log_dt)[:, None]
    mag = jnp.exp(a_re * dt)
    lb_re = mag * jnp.cos(a_im * dt)
    lb_im = mag * jnp.sin(a_im * dt)
    den = a_re * a_re + a_im * a_im
    nr, ni = lb_re - 1.0, lb_im
    f_re = (nr * a_re + ni * a_im) / den
    f_im = (ni * a_re - nr * a_im) / den
    bb_re = f_re[:, :, None] * b_re - f_im[:, :, None] * b_im
    bb_im = f_re[:, :, None] * b_im + f_im[:, :, None] * b_re
    gps = SSM_GROUPS // SSM_SLABS
    eye = jnp.eye(gps, dtype=F32)

    def b_slabs(bb):
        t = bb.transpose(0, 2, 1).reshape(SSM_SLABS, gps, SSM_GROUP, SSM_STATE)
        return jnp.einsum("mgcp,gh->mgchp", t, eye).reshape(SSM_SLABS, LANES, SLAB_STATE).astype(BF16)

    def c_slabs(c):
        t = c.transpose(0, 2, 1).reshape(SSM_SLABS, gps, SSM_STATE, SSM_GROUP)
        return jnp.einsum("mgpc,gh->mgphc", t, eye).reshape(SSM_SLABS, SLAB_STATE, LANES).astype(BF16)

    return dict(lb_re=lb_re.reshape(1, SSM_LANES), lb_im=lb_im.reshape(1, SSM_LANES),
                bd_re=b_slabs(bb_re), bd_im=b_slabs(bb_im),
                cd_re=c_slabs(c_re), cd_im_neg=c_slabs(-c_im),
                d=d.reshape(1, SSM_WIDTH), w_glu=w_glu.astype(BF16))


def _post_kernel(x_ref, oa_ref, os_ref, ga_ref, gs_ref, p_ref,
                 wba_ref, wbs_ref, wout_ref, wg_ref, wu_ref, wd_ref, wpg_ref, wpp_ref,
                 n_mix_post, n_ffn_pre, n_ffn_post, n_ple_pre, n_ple_post,
                 y_ref, *, nb, tm, ff_chunk):
    rows = nb * tm
    x = x_ref[...].reshape(rows, D_MODEL)
    oa = oa_ref[...].reshape(rows, SB_WIDTH)
    os_ = os_ref[...].reshape(rows, SSM_WIDTH)
    merged = (_sigmoid(ga_ref[...].reshape(rows, D_MODEL)) * _dot(oa, wba_ref[...])
              + _sigmoid(gs_ref[...].reshape(rows, D_MODEL)) * _dot(os_, wbs_ref[...]))
    x = x + _rms(_dot(merged.astype(BF16), wout_ref[...]), n_mix_post[...])

    f = _rms(x, n_ffn_pre[...]).astype(BF16)
    ff = jnp.zeros((rows, D_MODEL), F32)
    for c in range(D_FF // ff_chunk):
        cols = slice(c * ff_chunk, (c + 1) * ff_chunk)
        g = _dot(f, wg_ref[:, cols])
        a = (g * _sigmoid(g)) * _dot(f, wu_ref[:, cols])
        ff = ff + _dot(a.astype(BF16), wd_ref[cols, :])
    x = x + _rms(ff, n_ffn_post[...])

    gate = _sigmoid(_dot(_rms(x, n_ple_pre[...]).astype(BF16), wpg_ref[...]))
    pe = gate * _dot(p_ref[...].reshape(rows, PLE_DIM).astype(BF16), wpp_ref[...])
    y_ref[...] = (x + _rms(pe, n_ple_post[...])).reshape(nb, tm, D_MODEL)


def _post(x, o_attn, o_ssm, g_attn, g_ssm, p, W, *, nb, tm, ff_chunk=256):
    bsz, L, _ = x.shape
    grid = (bsz // nb, L // tm)
    tok = lambda width: pl.BlockSpec((nb, tm, width), lambda b, i: (b, i, 0))
    vec = _const_spec((1, D_MODEL))
    return pl.pallas_call(
        functools.partial(_post_kernel, nb=nb, tm=tm, ff_chunk=ff_chunk),
        grid=grid,
        in_specs=[tok(D_MODEL), tok(SB_WIDTH), tok(SSM_WIDTH),
                  tok(D_MODEL), tok(D_MODEL), tok(PLE_DIM),
                  _const_spec((SB_WIDTH, D_MODEL)), _const_spec((SSM_WIDTH, D_MODEL)),
                  _const_spec((D_MODEL, D_MODEL)),
                  _const_spec((D_MODEL, D_FF)), _const_spec((D_MODEL, D_FF)), _const_spec((D_FF, D_MODEL)),
                  _const_spec((D_MODEL, D_MODEL)), _const_spec((PLE_DIM, D_MODEL)),
                  vec, vec, vec, vec, vec],
        out_specs=tok(D_MODEL),
        out_shape=jax.ShapeDtypeStruct((bsz, L, D_MODEL), F32),
        compiler_params=pltpu.CompilerParams(
            dimension_semantics=("parallel", "parallel"), vmem_limit_bytes=VMEM_LIMIT),
        name="post",
    )(x, o_attn, o_ssm, g_attn, g_ssm, p,
      W["w_branch_attn"], W["w_branch_ssm"], W["w_out"], W["w_ffn_gate"], W["w_ffn_up"], W["w_ffn_down"],
      W["w_ple_gate"], W["w_ple_proj"],
      W["norm_mix_post"], W["norm_ffn_pre"], W["norm_ffn_post"], W["norm_ple_pre"], W["norm_ple_post"])


def _layer(x, p, cache, s_re0, s_im0, W, *, nb, tm, bq, tc, post_nb, post_tm):
    bsz, L, _ = x.shape
    q, k, v, u, g_attn, g_ssm = _inproj(x, W["norm_mix_pre"], W["w_in"], W["w_kv_t"], nb=nb, tm=tm,
                                           kv_t=cache is None)
    if cache is None:
        o_attn = _attention(q, k, v, bq=bq)
        k, v = (jnp.transpose(a, (0, 3, 1, 2)) for a in (k, v))
    else:
        o_attn = _attention_cached(q, k, v, *cache)
    o_ssm, s_re, s_im = _ssm(u, s_re0.reshape(bsz, SSM_LANES), s_im0.reshape(bsz, SSM_LANES), W["ssm"], tc=tc)
    y = _post(x, o_attn, o_ssm, g_attn, g_ssm, p, W, nb=post_nb, tm=post_tm)
    heads = (bsz, L, SB_HEADS, SB_HEAD_DIM)
    state = (bsz, SSM_GROUPS, SSM_STATE)
    return y, k.reshape(heads), v.reshape(heads), s_re.reshape(state), s_im.reshape(state)


def kernel(x_prompt, x_sample, cache_k, cache_v, state_ssm_re, state_ssm_im, p_prompt, p_sample, norm_mix_pre, norm_mix_post, w_in, ssm_a_re, ssm_a_im, ssm_log_dt, ssm_b_re, ssm_b_im, ssm_c_re, ssm_c_im, ssm_d, w_glu, w_branch_attn, w_branch_ssm, w_out, norm_ffn_pre, norm_ffn_post, w_ffn_gate, w_ffn_up, w_ffn_down, norm_ple_pre, norm_ple_post, w_ple_gate, w_ple_proj):
    depth = w_in.shape[0]
    yp, ys = x_prompt, x_sample
    outs = [[] for _ in range(8)]
    cache_kt, cache_vt = (jnp.transpose(c, (0, 1, 3, 4, 2)) for c in (cache_k, cache_v))
    for i in range(depth):
        W = dict(
            norm_mix_pre=norm_mix_pre[i][None], norm_mix_post=norm_mix_post[i][None],
            norm_ffn_pre=norm_ffn_pre[i][None], norm_ffn_post=norm_ffn_post[i][None],
            norm_ple_pre=norm_ple_pre[i][None], norm_ple_post=norm_ple_post[i][None],
            w_in=w_in[i].astype(BF16), w_kv_t=w_in[i][:, SB_WIDTH:3 * SB_WIDTH].T.astype(BF16),
            w_branch_attn=w_branch_attn[i].astype(BF16),
            w_branch_ssm=w_branch_ssm[i].astype(BF16), w_out=w_out[i].astype(BF16),
            w_ffn_gate=w_ffn_gate[i].astype(BF16), w_ffn_up=w_ffn_up[i].astype(BF16),
            w_ffn_down=w_ffn_down[i].astype(BF16), w_ple_gate=w_ple_gate[i].astype(BF16),
            w_ple_proj=w_ple_proj[i].astype(BF16),
            ssm=_ssm_params(ssm_a_re[i], ssm_a_im[i], ssm_log_dt[i], ssm_b_re[i], ssm_b_im[i],
                            ssm_c_re[i], ssm_c_im[i], ssm_d[i], w_glu[i]))
        bp, lp = yp.shape[0], yp.shape[1]
        bs, ls = ys.shape[0], ys.shape[1]
        zero_state = jnp.zeros((bp, SSM_GROUPS, SSM_STATE), F32)
        yp, kp, vp, srp, sip = _layer(
            yp, p_prompt[i], None, zero_state, zero_state, W,
            nb=1, tm=min(512, lp), bq=min(256, lp), tc=512 // bp, post_nb=1, post_tm=min(512, lp))
        ys, kn, vn, srs, sis = _layer(
            ys, p_sample[i], (cache_kt, cache_vt, i), state_ssm_re[i], state_ssm_im[i], W,
            nb=512 // ls, tm=ls, bq=ls, tc=512 // bs, post_nb=512 // ls, post_tm=ls)
        for lst, val in zip(outs, (kp, vp, srp, sip, kn, vn, srs, sis)):
            lst.append(val)
    return (yp, ys) + tuple(jnp.stack(o) for o in outs)
```

```python
import functools
import math

import jax
import jax.numpy as jnp
from jax import lax
from jax.experimental import pallas as pl
from jax.experimental.pallas import tpu as pltpu

F32 = jnp.float32
BF16 = jnp.bfloat16

D_MODEL = 1024
PLE_DIM = 256
SB_WIDTH = 512
SB_HEAD_DIM = 64
SB_HEADS = 8
SSM_WIDTH = 512
SSM_GROUP = 16
SSM_GROUPS = 32
SSM_STATE = 64
SSM_LANES = SSM_GROUPS * SSM_STATE
D_FF = 2816
IN_WIDTH = 3 * SB_WIDTH + SSM_WIDTH + 2 * D_MODEL
RMS_EPS = 1e-6

LANES = 128
HEADS_PER_BLOCK = LANES // SB_HEAD_DIM
SSM_SLABS = SSM_WIDTH // LANES
SLAB_STATE = SSM_LANES // SSM_SLABS
VMEM_LIMIT = 56 * 1024 * 1024


def _const_spec(shape):
    nd = len(shape)
    return pl.BlockSpec(shape, lambda *_: (0,) * nd, pipeline_mode=pl.Buffered(1))


def _rms(x, gain):
    ms = jnp.mean(x * x, axis=-1, keepdims=True)
    return x * lax.rsqrt(ms + RMS_EPS) * gain


def _sigmoid(x):
    return 0.5 * jnp.tanh(0.5 * x) + 0.5


def _dot(a, b):
    return jnp.dot(a, b, preferred_element_type=F32)


def _dot_nt(a, b):
    return lax.dot_general(a, b, (((1,), (1,)), ((), ())), preferred_element_type=F32)


def _inproj_kernel(*refs, nb, tm, kv_t):
    if kv_t:
        x_ref, gain_ref, w_ref, wkvt_ref, q_ref, k_ref, v_ref, u_ref, ga_ref, gs_ref = refs
    else:
        x_ref, gain_ref, w_ref, q_ref, k_ref, v_ref, u_ref, ga_ref, gs_ref = refs
    x = x_ref[...].reshape(nb * tm, D_MODEL)
    h = _rms(x, gain_ref[...]).astype(BF16)

    def proj(lo, width):
        return _dot(h, w_ref[:, lo:lo + width])

    q_ref[...] = (proj(0, SB_WIDTH) * (SB_HEAD_DIM ** -0.5)).astype(BF16).reshape(nb, tm, SB_WIDTH)
    if kv_t:
        k_ref[...] = _dot_nt(wkvt_ref[0:SB_WIDTH, :], h).reshape(SB_HEADS, SB_HEAD_DIM, tm)
        v_ref[...] = _dot_nt(wkvt_ref[SB_WIDTH:2 * SB_WIDTH, :], h).reshape(SB_HEADS, SB_HEAD_DIM, tm)
    else:
        k_ref[...] = proj(SB_WIDTH, SB_WIDTH).reshape(nb, tm, SB_WIDTH)
        v_ref[...] = proj(2 * SB_WIDTH, SB_WIDTH).reshape(nb, tm, SB_WIDTH)
    u_ref[...] = proj(3 * SB_WIDTH, SSM_WIDTH).reshape(nb, tm, SSM_WIDTH)
    off = 3 * SB_WIDTH + SSM_WIDTH
    ga_ref[...] = proj(off, D_MODEL).reshape(nb, tm, D_MODEL)
    gs_ref[...] = proj(off + D_MODEL, D_MODEL).reshape(nb, tm, D_MODEL)


def _inproj(x, gain, w_in, w_kv_t, *, nb, tm, kv_t):
    bsz, L, _ = x.shape
    grid = (bsz // nb, L // tm)
    tok = lambda width: pl.BlockSpec((nb, tm, width), lambda b, i: (b, i, 0))
    in_specs = [tok(D_MODEL), _const_spec((1, D_MODEL)), _const_spec((D_MODEL, IN_WIDTH))]
    args = [x, gain, w_in]
    if kv_t:
        assert nb == 1
        in_specs.append(_const_spec((2 * SB_WIDTH, D_MODEL)))
        args.append(w_kv_t)
        kv_spec = pl.BlockSpec((None, SB_HEADS, SB_HEAD_DIM, tm), lambda b, i: (b, 0, 0, i))
        kv_shape = jax.ShapeDtypeStruct((bsz, SB_HEADS, SB_HEAD_DIM, L), F32)
    else:
        kv_spec = tok(SB_WIDTH)
        kv_shape = jax.ShapeDtypeStruct((bsz, L, SB_WIDTH), F32)
    return pl.pallas_call(
        functools.partial(_inproj_kernel, nb=nb, tm=tm, kv_t=kv_t),
        grid=grid,
        in_specs=in_specs,
        out_specs=[tok(SB_WIDTH), kv_spec, kv_spec, tok(SSM_WIDTH), tok(D_MODEL), tok(D_MODEL)],
        out_shape=[jax.ShapeDtypeStruct((bsz, L, SB_WIDTH), BF16), kv_shape, kv_shape,
                   jax.ShapeDtypeStruct((bsz, L, SSM_WIDTH), F32),
                   jax.ShapeDtypeStruct((bsz, L, D_MODEL), F32),
                   jax.ShapeDtypeStruct((bsz, L, D_MODEL), F32)],
        compiler_params=pltpu.CompilerParams(
            dimension_semantics=("parallel", "parallel"), vmem_limit_bytes=VMEM_LIMIT),
        name="inproj",
    )(*args)


LOG2E = 1.0 / math.log(2.0)
SKIP_LOG2 = 105.0 * LOG2E


SOFTPLUS2_LINEAR = 100.0


def _softplus2(y):
    return jnp.maximum(y, jnp.log2(1.0 + jnp.exp2(jnp.minimum(y, SOFTPLUS2_LINEAR))))


def _suffix_sums(sp, tri):
    return _dot(sp.astype(BF16), tri)


def _sb_weights(y, cum, c, visible):
    w = jnp.exp2(y - cum - c)
    if visible is not None:
        w = jnp.where(visible, w, 0.0)
    return w.astype(BF16)


def _unfinished(*cs):
    m = jnp.min(cs[0])
    for c in cs[1:]:
        m = jnp.minimum(m, jnp.min(c))
    return (m < SKIP_LOG2).astype(jnp.int32)


def _more(st):
    return (st[0] >= 0) & (st[1] > 0)


NEAR_ROWS_EIGHTHS = 5


def _attn_kernel(q_ref, k_ref, v_ref, o_ref, c_ref, acc_ref, *, bq):
    i = pl.program_id(1)
    n_pairs = SB_WIDTH // LANES
    lane = lax.broadcasted_iota(jnp.int32, (bq, LANES), 1)
    row = lax.broadcasted_iota(jnp.int32, (bq, bq), 0)
    col = lax.broadcasted_iota(jnp.int32, (bq, bq), 1)
    tri = (row >= col).astype(BF16)
    diag_visible = col < row

    in_head = [(lane >= h * SB_HEAD_DIM) & (lane < (h + 1) * SB_HEAD_DIM) for h in range(HEADS_PER_BLOCK)]
    qs = []
    for p in range(n_pairs):
        qp = q_ref[:, p * LANES:(p + 1) * LANES]
        qs += [jnp.where(m, qp, jnp.zeros_like(qp)) for m in in_head]

    split = bq * NEAR_ROWS_EIGHTHS // 8
    top, bottom, whole = slice(0, split), slice(split, bq), slice(0, bq)

    def visit(parts, first):
        chains = []
        for j, rows, cols, masked in parts:
            s0 = pl.multiple_of(j * bq, bq) + cols.start
            width = cols.stop - cols.start
            for p in range(n_pairs):
                heads = slice(p * HEADS_PER_BLOCK, (p + 1) * HEADS_PER_BLOCK)
                kt = k_ref[heads, :, pl.ds(s0, width)].reshape(LANES, width).astype(BF16)
                vt = v_ref[heads, :, pl.ds(s0, width)].reshape(LANES, width).astype(BF16)
                chains += [(h, rows, kt, vt, tri[:width, :width], diag_visible[rows, cols] if masked else None)
                           for h in range(heads.start, heads.stop)]
        ys = [_dot(qs[h][rows], kt) * LOG2E for h, rows, kt, _, _, _ in chains]
        sps = [_softplus2(y) if ch[5] is None else jnp.where(ch[5], _softplus2(y), 0.0) for y, ch in zip(ys, chains)]
        cums = [_suffix_sums(sp, ch[4]) for sp, ch in zip(sps, chains)]
        cs_old = [0.0 if first else c_ref[h, rows] for h, rows, *_ in chains]
        ws = [_sb_weights(y, cum, c, ch[5]) for y, cum, c, ch in zip(ys, cums, cs_old, chains)]
        accs = [_dot_nt(w, ch[3]) for w, ch in zip(ws, chains)]
        for (h, rows, *_), acc, cum, c in zip(chains, accs, cums, cs_old):
            acc_ref[h, rows] = acc if first else acc_ref[h, rows] + acc
            c_ref[h, rows] = cum[:, 0:1] + c

    visit([(i, whole, whole, True)], True)

    @pl.when(i > 0)
    def _():
        visit([(i - 1, top, whole, False)], False)

        @pl.when(_unfinished(c_ref[:, bottom]) > 0)
        def _():
            visit([(i - 1, bottom, whole, False)], False)

    def earlier(st):
        visit([(st[0], whole, whole, False)], False)
        return st[0] - 1, _unfinished(c_ref[...])

    lax.while_loop(_more, earlier, (i - 2, _unfinished(c_ref[...])))
    for p in range(n_pairs):
        o_ref[:, p * LANES:(p + 1) * LANES] = jnp.where(
            in_head[0], acc_ref[p * HEADS_PER_BLOCK], acc_ref[p * HEADS_PER_BLOCK + 1]).astype(o_ref.dtype)


def _attention(q, k_t, v_t, *, bq):
    bsz, L, _ = q.shape
    qspec = pl.BlockSpec((None, bq, SB_WIDTH), lambda b, i: (b, i, 0))
    kvspec = pl.BlockSpec((None, SB_HEADS, SB_HEAD_DIM, L), lambda b, i: (b, 0, 0, 0))
    return pl.pallas_call(
        functools.partial(_attn_kernel, bq=bq),
        grid=(bsz, L // bq),
        in_specs=[qspec, kvspec, kvspec],
        out_specs=qspec,
        out_shape=jax.ShapeDtypeStruct((bsz, L, SB_WIDTH), BF16),
        scratch_shapes=[pltpu.VMEM((SB_HEADS, bq, 1), F32), pltpu.VMEM((SB_HEADS, bq, LANES), F32)],
        compiler_params=pltpu.CompilerParams(
            dimension_semantics=("parallel", "parallel"), vmem_limit_bytes=VMEM_LIMIT),
        name="sb_attention",
    )(q, k_t, v_t)


def _attn_cached_kernel(q_ref, k_ref, v_ref, kl_ref, vl_ref, kc_hbm, vc_hbm, o_ref, kbuf, vbuf, sem,
                        *, layer, lq, bk, n_past):
    b = pl.program_id(0)
    hd = SB_HEAD_DIM
    q = q_ref[...]
    kn = k_ref[...].astype(BF16)
    vn = v_ref[...].astype(BF16)
    head = lambda x, h: x[:, h * hd:(h + 1) * hd]
    qs = [head(q, h) for h in range(SB_HEADS)]
    rows = SB_HEADS * lq

    r = lax.broadcasted_iota(jnp.int32, (rows, lq), 0) % lq
    s = lax.broadcasted_iota(jnp.int32, (rows, lq), 1)
    visible = s < r
    tr = lax.broadcasted_iota(jnp.int32, (lq, lq), 0)
    tc = lax.broadcasted_iota(jnp.int32, (lq, lq), 1)
    yd = jnp.concatenate([_dot_nt(qs[h], head(kn, h)) for h in range(SB_HEADS)], axis=0) * LOG2E
    cumd = _suffix_sums(jnp.where(visible, _softplus2(yd), 0.0), (tr >= tc).astype(BF16))
    wd = _sb_weights(yd, cumd, 0.0, visible)
    accs = [_dot(wd[h * lq:(h + 1) * lq], head(vn, h)) for h in range(SB_HEADS)]
    c = cumd[:, 0:1]

    pr = lax.broadcasted_iota(jnp.int32, (bk, bk), 0)
    pc = lax.broadcasted_iota(jnp.int32, (bk, bk), 1)
    tri_p = (pr >= pc).astype(BF16)

    def cache_block(kref, vref, c, accs):
        y = jnp.concatenate([_dot(qs[h], kref[h].astype(BF16)) for h in range(SB_HEADS)], axis=0) * LOG2E
        cum = _suffix_sums(_softplus2(y), tri_p)
        w = _sb_weights(y, cum, c, None)
        accs = [accs[h] + _dot_nt(w[h * lq:(h + 1) * lq], vref[h].astype(BF16)) for h in range(SB_HEADS)]
        return c + cum[:, 0:1], accs

    c, accs = cache_block(kl_ref, vl_ref, c, accs)

    def fetch(j):
        s0 = pl.multiple_of(j * bk, bk)
        return (pltpu.make_async_copy(kc_hbm.at[layer, b, :, :, pl.ds(s0, bk)], kbuf, sem.at[0]),
                pltpu.make_async_copy(vc_hbm.at[layer, b, :, :, pl.ds(s0, bk)], vbuf, sem.at[1]))

    def visit(st):
        j, c, accs = st[0], st[2], list(st[3:])
        copies = fetch(j)
        for cp in copies:
            cp.start()
        for cp in copies:
            cp.wait()
        c, accs = cache_block(kbuf, vbuf, c, accs)
        return (j - 1, _unfinished(c), c) + tuple(accs)

    st = lax.while_loop(_more, visit, (n_past - 2, _unfinished(c), c) + tuple(accs))
    o_ref[...] = jnp.concatenate(st[3:], axis=1).astype(o_ref.dtype)


def _attention_cached(q, k, v, cache_kt, cache_vt, layer, *, bk=256):
    bsz, lq, _ = q.shape
    past = cache_kt.shape[4]
    n_past = past // bk
    tok = pl.BlockSpec((None, lq, SB_WIDTH), lambda b: (b, 0, 0))
    last = pl.BlockSpec((None, None, SB_HEADS, SB_HEAD_DIM, bk), lambda b: (layer, b, 0, 0, n_past - 1))
    hbm = pl.BlockSpec(memory_space=pl.ANY)
    return pl.pallas_call(
        functools.partial(_attn_cached_kernel, layer=layer, lq=lq, bk=bk, n_past=n_past),
        grid=(bsz,),
        in_specs=[tok, tok, tok, last, last, hbm, hbm],
        out_specs=tok,
        out_shape=jax.ShapeDtypeStruct((bsz, lq, SB_WIDTH), BF16),
        scratch_shapes=[pltpu.VMEM((SB_HEADS, SB_HEAD_DIM, bk), F32), pltpu.VMEM((SB_HEADS, SB_HEAD_DIM, bk), F32),
                        pltpu.SemaphoreType.DMA((2,))],
        compiler_params=pltpu.CompilerParams(
            dimension_semantics=("parallel",), vmem_limit_bytes=VMEM_LIMIT),
        name="sb_attention_cached",
    )(q, k, v, cache_kt, cache_vt, cache_kt, cache_vt)


def _gelu_tanh(x):
    return 0.5 * x * (1.0 + jnp.tanh(math.sqrt(2.0 / math.pi) * (x + 0.044715 * (x * x * x))))


SSM_ROW_PARTS = 4


def _ssm_kernel(u_ref, s0re_ref, s0im_ref, lbre_ref, lbim_ref, bre_ref, bim_ref, cre_ref, cimn_ref,
                d_ref, wglu_ref, o_ref, sfre_ref, sfim_ref, sre, sim, st_re, st_im, tmaj, ybuf, *, bsz, tc):
    step = pl.program_id(0)
    rows = tc * bsz
    part_rows = rows // SSM_ROW_PARTS
    part_steps = tc // SSM_ROW_PARTS

    @pl.when(step == 0)
    def _():
        st_re[...] = s0re_ref[...]
        st_im[...] = s0im_ref[...]

    for b in range(bsz):
        for m in range(SSM_SLABS):
            tmaj[m, pl.ds(b, tc, stride=bsz), :] = u_ref[b, :, m * LANES:(m + 1) * LANES]

    def part(q):
        return slice(q * part_rows, (q + 1) * part_rows)

    def slab(m):
        return slice(m * SLAB_STATE, (m + 1) * SLAB_STATE)

    def b_proj(m, q):
        um = tmaj[m, part(q), :].astype(BF16)
        sre[part(q), slab(m)] = _dot(um, bre_ref[m])
        sim[part(q), slab(m)] = _dot(um, bim_ref[m])

    def recurrence(m, q):
        lr = jnp.broadcast_to(lbre_ref[:, slab(m)], (8, SLAB_STATE))
        li = jnp.broadcast_to(lbim_ref[:, slab(m)], (8, SLAB_STATE))
        for sb in range(bsz // 8):
            subl = slice(sb * 8, (sb + 1) * 8)
            sr, si = st_re[subl, slab(m)], st_im[subl, slab(m)]
            for t in range(q * part_steps, (q + 1) * part_steps):
                r = slice(t * bsz + sb * 8, t * bsz + sb * 8 + 8)
                sr, si = (lr * sr - li * si + sre[r, slab(m)], lr * si + li * sr + sim[r, slab(m)])
                sre[r, slab(m)] = sr
                sim[r, slab(m)] = si
            st_re[subl, slab(m)] = sr
            st_im[subl, slab(m)] = si

    def c_proj(m, q):
        ybuf[m, part(q), :] = (_dot(sre[part(q), slab(m)].astype(BF16), cre_ref[m])
                               + _dot(sim[part(q), slab(m)].astype(BF16), cimn_ref[m]))

    for q in range(SSM_ROW_PARTS):
        b_proj(0, q)
    for m in range(SSM_SLABS):
        for q in range(SSM_ROW_PARTS):
            recurrence(m, q)
            if m + 1 < SSM_SLABS:
                b_proj(m + 1, q)
            c_proj(m, q)

    u = jnp.concatenate([tmaj[m] for m in range(SSM_SLABS)], axis=1)
    y = jnp.concatenate([ybuf[m] for m in range(SSM_SLABS)], axis=1) + d_ref[...] * u
    y = _gelu_tanh(y)
    o = y * _sigmoid(_dot(y.astype(BF16), wglu_ref[...]))
    for m in range(SSM_SLABS):
        tmaj[m] = o[:, m * LANES:(m + 1) * LANES]
    for b in range(bsz):
        for m in range(SSM_SLABS):
            o_ref[b, :, m * LANES:(m + 1) * LANES] = tmaj[m, pl.ds(b, tc, stride=bsz), :].astype(o_ref.dtype)

    @pl.when(step == pl.num_programs(0) - 1)
    def _():
        sfre_ref[...] = st_re[...]
        sfim_ref[...] = st_im[...]


def _ssm(u, s0_re, s0_im, prm, *, tc):
    bsz, L, _ = u.shape
    blk = tc * bsz
    row_spec = pl.BlockSpec((bsz, tc, SSM_WIDTH), lambda s: (0, s, 0))
    st_shape = (bsz, SSM_LANES)
    return pl.pallas_call(
        functools.partial(_ssm_kernel, bsz=bsz, tc=tc),
        grid=(L // tc,),
        in_specs=[row_spec, _const_spec(st_shape), _const_spec(st_shape),
                  _const_spec((1, SSM_LANES)), _const_spec((1, SSM_LANES)),
                  _const_spec((SSM_SLABS, LANES, SLAB_STATE)), _const_spec((SSM_SLABS, LANES, SLAB_STATE)),
                  _const_spec((SSM_SLABS, SLAB_STATE, LANES)), _const_spec((SSM_SLABS, SLAB_STATE, LANES)),
                  _const_spec((1, SSM_WIDTH)), _const_spec((SSM_WIDTH, SSM_WIDTH))],
        out_specs=[row_spec, _const_spec(st_shape), _const_spec(st_shape)],
        out_shape=[jax.ShapeDtypeStruct((bsz, L, SSM_WIDTH), BF16),
                   jax.ShapeDtypeStruct(st_shape, F32), jax.ShapeDtypeStruct(st_shape, F32)],
        scratch_shapes=[pltpu.VMEM((blk, SSM_LANES), F32), pltpu.VMEM((blk, SSM_LANES), F32),
                        pltpu.VMEM(st_shape, F32), pltpu.VMEM(st_shape, F32),
                        pltpu.VMEM((SSM_SLABS, blk, LANES), F32),
                        pltpu.VMEM((SSM_SLABS, blk, LANES), F32)],
        compiler_params=pltpu.CompilerParams(
            dimension_semantics=("arbitrary",), vmem_limit_bytes=VMEM_LIMIT),
        name="s5_scan",
    )(u, s0_re, s0_im, prm["lb_re"], prm["lb_im"], prm["bd_re"], prm["bd_im"],
      prm["cd_re"], prm["cd_im_neg"], prm["d"], prm["w_glu"])


def _ssm_params(a_re, a_im, log_dt, b_re, b_im, c_re, c_im, d, w_glu):
    dt = jnp.exp(log_dt)[:, None]
    mag = jnp.exp(a_re * dt)
    lb_re = mag * jnp.cos(a_im * dt)
    lb_im = mag * jnp.sin(a_im * dt)
    den = a_re * a_re + a_im * a_im
    nr, ni = lb_re - 1.0, lb_im
    f_re = (nr * a_re + ni * a_im) / den
    f_im = (ni * a_re - nr * a_im) / den
    bb_re = f_re[:, :, None] * b_re - f_im[:, :, None] * b_im
    bb_im = f_re[:, :, None] * b_im + f_im[:, :, None] * b_re
    gps = SSM_GROUPS // SSM_SLABS
    eye = jnp.eye(gps, dtype=F32)

    def b_slabs(bb):
        t = bb.transpose(0, 2, 1).reshape(SSM_SLABS, gps, SSM_GROUP, SSM_STATE)
        return jnp.einsum("mgcp,gh->mgchp", t, eye).reshape(SSM_SLABS, LANES, SLAB_STATE).astype(BF16)

    def c_slabs(c):
        t = c.transpose(0, 2, 1).reshape(SSM_SLABS, gps, SSM_STATE, SSM_GROUP)
        return jnp.einsum("mgpc,gh->mgphc", t, eye).reshape(SSM_SLABS, SLAB_STATE, LANES).astype(BF16)

    return dict(lb_re=lb_re.reshape(1, SSM_LANES), lb_im=lb_im.reshape(1, SSM_LANES),
                bd_re=b_slabs(bb_re), bd_im=b_slabs(bb_im),
                cd_re=c_slabs(c_re), cd_im_neg=c_slabs(-c_im),
                d=d.reshape(1, SSM_WIDTH), w_glu=w_glu.astype(BF16))


def _post_kernel(x_ref, oa_ref, os_ref, ga_ref, gs_ref, p_ref,
                 wba_ref, wbs_ref, wout_ref, wg_ref, wu_ref, wd_ref, wpg_ref, wpp_ref,
                 n_mix_post, n_ffn_pre, n_ffn_post, n_ple_pre, n_ple_post,
                 y_ref, *, nb, tm, ff_chunk):
    rows = nb * tm
    x = x_ref[...].reshape(rows, D_MODEL)
    oa = oa_ref[...].reshape(rows, SB_WIDTH)
    os_ = os_ref[...].reshape(rows, SSM_WIDTH)
    merged = (_sigmoid(ga_ref[...].reshape(rows, D_MODEL)) * _dot(oa, wba_ref[...])
              + _sigmoid(gs_ref[...].reshape(rows, D_MODEL)) * _dot(os_, wbs_ref[...]))
    x = x + _rms(_dot(merged.astype(BF16), wout_ref[...]), n_mix_post[...])

    f = _rms(x, n_ffn_pre[...]).astype(BF16)
    ff = jnp.zeros((rows, D_MODEL), F32)
    for c in range(D_FF // ff_chunk):
        cols = slice(c * ff_chunk, (c + 1) * ff_chunk)
        g = _dot(f, wg_ref[:, cols])
        a = (g * _sigmoid(g)) * _dot(f, wu_ref[:, cols])
        ff = ff + _dot(a.astype(BF16), wd_ref[cols, :])
    x = x + _rms(ff, n_ffn_post[...])

    gate = _sigmoid(_dot(_rms(x, n_ple_pre[...]).astype(BF16), wpg_ref[...]))
    pe = gate * _dot(p_ref[...].reshape(rows, PLE_DIM).astype(BF16), wpp_ref[...])
    y_ref[...] = (x + _rms(pe, n_ple_post[...])).reshape(nb, tm, D_MODEL)


def _post(x, o_attn, o_ssm, g_attn, g_ssm, p, W, *, nb, tm, ff_chunk=256):
    bsz, L, _ = x.shape
    grid = (bsz // nb, L // tm)
    tok = lambda width: pl.BlockSpec((nb, tm, width), lambda b, i: (b, i, 0))
    vec = _const_spec((1, D_MODEL))
    return pl.pallas_call(
        functools.partial(_post_kernel, nb=nb, tm=tm, ff_chunk=ff_chunk),
        grid=grid,
        in_specs=[tok(D_MODEL), tok(SB_WIDTH), tok(SSM_WIDTH),
                  tok(D_MODEL), tok(D_MODEL), tok(PLE_DIM),
                  _const_spec((SB_WIDTH, D_MODEL)), _const_spec((SSM_WIDTH, D_MODEL)),
                  _const_spec((D_MODEL, D_MODEL)),
                  _const_spec((D_MODEL, D_FF)), _const_spec((D_MODEL, D_FF)), _const_spec((D_FF, D_MODEL)),
                  _const_spec((D_MODEL, D_MODEL)), _const_spec((PLE_DIM, D_MODEL)),
                  vec, vec, vec, vec, vec],
        out_specs=tok(D_MODEL),
        out_shape=jax.ShapeDtypeStruct((bsz, L, D_MODEL), F32),
        compiler_params=pltpu.CompilerParams(
            dimension_semantics=("parallel", "parallel"), vmem_limit_bytes=VMEM_LIMIT),
        name="post",
    )(x, o_attn, o_ssm, g_attn, g_ssm, p,
      W["w_branch_attn"], W["w_branch_ssm"], W["w_out"], W["w_ffn_gate"], W["w_ffn_up"], W["w_ffn_down"],
      W["w_ple_gate"], W["w_ple_proj"],
      W["norm_mix_post"], W["norm_ffn_pre"], W["norm_ffn_post"], W["norm_ple_pre"], W["norm_ple_post"])


def _layer(x, p, cache, s_re0, s_im0, W, *, nb, tm, bq, tc, post_nb, post_tm):
    bsz, L, _ = x.shape
    q, k, v, u, g_attn, g_ssm = _inproj(x, W["norm_mix_pre"], W["w_in"], W["w_kv_t"], nb=nb, tm=tm,
                                           kv_t=cache is None)
    if cache is None:
        o_attn = _attention(q, k, v, bq=bq)
        k, v = (jnp.transpose(a, (0, 3, 1, 2)) for a in (k, v))
    else:
        o_attn = _attention_cached(q, k, v, *cache)
    o_ssm, s_re, s_im = _ssm(u, s_re0.reshape(bsz, SSM_LANES), s_im0.reshape(bsz, SSM_LANES), W["ssm"], tc=tc)
    y = _post(x, o_attn, o_ssm, g_attn, g_ssm, p, W, nb=post_nb, tm=post_tm)
    heads = (bsz, L, SB_HEADS, SB_HEAD_DIM)
    state = (bsz, SSM_GROUPS, SSM_STATE)
    return y, k.reshape(heads), v.reshape(heads), s_re.reshape(state), s_im.reshape(state)


def kernel(x_prompt, x_sample, cache_k, cache_v, state_ssm_re, state_ssm_im, p_prompt, p_sample, norm_mix_pre, norm_mix_post, w_in, ssm_a_re, ssm_a_im, ssm_log_dt, ssm_b_re, ssm_b_im, ssm_c_re, ssm_c_im, ssm_d, w_glu, w_branch_attn, w_branch_ssm, w_out, norm_ffn_pre, norm_ffn_post, w_ffn_gate, w_ffn_up, w_ffn_down, norm_ple_pre, norm_ple_post, w_ple_gate, w_ple_proj):
    depth = w_in.shape[0]
    yp, ys = x_prompt, x_sample
    outs = [[] for _ in range(8)]
    cache_kt, cache_vt = (jnp.transpose(c, (0, 1, 3, 4, 2)) for c in (cache_k, cache_v))
    for i in range(depth):
        W = dict(
            norm_mix_pre=norm_mix_pre[i][None], norm_mix_post=norm_mix_post[i][None],
            norm_ffn_pre=norm_ffn_pre[i][None], norm_ffn_post=norm_ffn_post[i][None],
            norm_ple_pre=norm_ple_pre[i][None], norm_ple_post=norm_ple_post[i][None],
            w_in=w_in[i].astype(BF16), w_kv_t=w_in[i][:, SB_WIDTH:3 * SB_WIDTH].T.astype(BF16),
            w_branch_attn=w_branch_attn[i].astype(BF16),
            w_branch_ssm=w_branch_ssm[i].astype(BF16), w_out=w_out[i].astype(BF16),
            w_ffn_gate=w_ffn_gate[i].astype(BF16), w_ffn_up=w_ffn_up[i].astype(BF16),
            w_ffn_down=w_ffn_down[i].astype(BF16), w_ple_gate=w_ple_gate[i].astype(BF16),
            w_ple_proj=w_ple_proj[i].astype(BF16),
            ssm=_ssm_params(ssm_a_re[i], ssm_a_im[i], ssm_log_dt[i], ssm_b_re[i], ssm_b_im[i],
                            ssm_c_re[i], ssm_c_im[i], ssm_d[i], w_glu[i]))
        bp, lp = yp.shape[0], yp.shape[1]
        bs, ls = ys.shape[0], ys.shape[1]
        zero_state = jnp.zeros((bp, SSM_GROUPS, SSM_STATE), F32)
        yp, kp, vp, srp, sip = _layer(
            yp, p_prompt[i], None, zero_state, zero_state, W,
            nb=1, tm=min(512, lp), bq=min(256, lp), tc=512 // bp, post_nb=1, post_tm=min(512, lp))
        ys, kn, vn, srs, sis = _layer(
            ys, p_sample[i], (cache_kt, cache_vt, i), state_ssm_re[i], state_ssm_im[i], W,
            nb=512 // ls, tm=ls, bq=ls, tc=512 // bs, post_nb=512 // ls, post_tm=ls)
        for lst, val in zip(outs, (kp, vp, srp, sip, kn, vn, srs, sis)):
            lst.append(val)
    return (yp, ys) + tuple(jnp.stack(o) for o in outs)
```

```python
import functools
import math

import jax
import jax.numpy as jnp
from jax import lax
from jax.experimental import pallas as pl
from jax.experimental.pallas import tpu as pltpu

F32 = jnp.float32
BF16 = jnp.bfloat16

D_MODEL = 1024
PLE_DIM = 256
SB_WIDTH = 512
SB_HEAD_DIM = 64
SB_HEADS = 8
SSM_WIDTH = 512
SSM_GROUP = 16
SSM_GROUPS = 32
SSM_STATE = 64
SSM_LANES = SSM_GROUPS * SSM_STATE
D_FF = 2816
IN_WIDTH = 3 * SB_WIDTH + SSM_WIDTH + 2 * D_MODEL
RMS_EPS = 1e-6

LANES = 128
HEADS_PER_BLOCK = LANES // SB_HEAD_DIM
SSM_SLABS = SSM_WIDTH // LANES
SLAB_STATE = SSM_LANES // SSM_SLABS
VMEM_LIMIT = 56 * 1024 * 1024


def _const_spec(shape):
    nd = len(shape)
    return pl.BlockSpec(shape, lambda *_: (0,) * nd, pipeline_mode=pl.Buffered(1))


def _rms(x, gain):
    ms = jnp.mean(x * x, axis=-1, keepdims=True)
    return x * lax.rsqrt(ms + RMS_EPS) * gain


def _sigmoid(x):
    return 0.5 * jnp.tanh(0.5 * x) + 0.5


def _dot(a, b):
    return jnp.dot(a, b, preferred_element_type=F32)


def _dot_nt(a, b):
    return lax.dot_general(a, b, (((1,), (1,)), ((), ())), preferred_element_type=F32)


def _inproj_kernel(*refs, nb, tm, kv_t):
    if kv_t:
        x_ref, gain_ref, w_ref, wkvt_ref, q_ref, k_ref, v_ref, u_ref, ga_ref, gs_ref = refs
    else:
        x_ref, gain_ref, w_ref, q_ref, k_ref, v_ref, u_ref, ga_ref, gs_ref = refs
    x = x_ref[...].reshape(nb * tm, D_MODEL)
    h = _rms(x, gain_ref[...]).astype(BF16)

    def proj(lo, width):
        return _dot(h, w_ref[:, lo:lo + width])

    q_ref[...] = (proj(0, SB_WIDTH) * (SB_HEAD_DIM ** -0.5)).astype(BF16).reshape(nb, tm, SB_WIDTH)
    if kv_t:
        k_ref[...] = _dot_nt(wkvt_ref[0:SB_WIDTH, :], h).reshape(SB_HEADS, SB_HEAD_DIM, tm)
        v_ref[...] = _dot_nt(wkvt_ref[SB_WIDTH:2 * SB_WIDTH, :], h).reshape(SB_HEADS, SB_HEAD_DIM, tm)
    else:
        k_ref[...] = proj(SB_WIDTH, SB_WIDTH).reshape(nb, tm, SB_WIDTH)
        v_ref[...] = proj(2 * SB_WIDTH, SB_WIDTH).reshape(nb, tm, SB_WIDTH)
    u_ref[...] = proj(3 * SB_WIDTH, SSM_WIDTH).reshape(nb, tm, SSM_WIDTH)
    off = 3 * SB_WIDTH + SSM_WIDTH
    ga_ref[...] = proj(off, D_MODEL).reshape(nb, tm, D_MODEL)
    gs_ref[...] = proj(off + D_MODEL, D_MODEL).reshape(nb, tm, D_MODEL)


def _inproj(x, gain, w_in, w_kv_t, *, nb, tm, kv_t):
    bsz, L, _ = x.shape
    grid = (bsz // nb, L // tm)
    tok = lambda width: pl.BlockSpec((nb, tm, width), lambda b, i: (b, i, 0))
    in_specs = [tok(D_MODEL), _const_spec((1, D_MODEL)), _const_spec((D_MODEL, IN_WIDTH))]
    args = [x, gain, w_in]
    if kv_t:
        assert nb == 1
        in_specs.append(_const_spec((2 * SB_WIDTH, D_MODEL)))
        args.append(w_kv_t)
        kv_spec = pl.BlockSpec((None, SB_HEADS, SB_HEAD_DIM, tm), lambda b, i: (b, 0, 0, i))
        kv_shape = jax.ShapeDtypeStruct((bsz, SB_HEADS, SB_HEAD_DIM, L), F32)
    else:
        kv_spec = tok(SB_WIDTH)
        kv_shape = jax.ShapeDtypeStruct((bsz, L, SB_WIDTH), F32)
    return pl.pallas_call(
        functools.partial(_inproj_kernel, nb=nb, tm=tm, kv_t=kv_t),
        grid=grid,
        in_specs=in_specs,
        out_specs=[tok(SB_WIDTH), kv_spec, kv_spec, tok(SSM_WIDTH), tok(D_MODEL), tok(D_MODEL)],
        out_shape=[jax.ShapeDtypeStruct((bsz, L, SB_WIDTH), BF16), kv_shape, kv_shape,
                   jax.ShapeDtypeStruct((bsz, L, SSM_WIDTH), F32),
                   jax.ShapeDtypeStruct((bsz, L, D_MODEL), F32),
                   jax.ShapeDtypeStruct((bsz, L, D_MODEL), F32)],
        compiler_params=pltpu.CompilerParams(
            dimension_semantics=("parallel", "parallel"), vmem_limit_bytes=VMEM_LIMIT),
        name="inproj",
    )(*args)


LOG2E = 1.0 / math.log(2.0)
SKIP_LOG2 = 105.0 * LOG2E


SOFTPLUS2_LINEAR = 100.0


def _softplus2(y):
    return jnp.maximum(y, jnp.log2(1.0 + jnp.exp2(jnp.minimum(y, SOFTPLUS2_LINEAR))))


def _suffix_sums(sp, tri):
    return _dot(sp.astype(BF16), tri)


def _sb_weights(y, cum, c, visible):
    w = jnp.exp2(y - cum - c)
    if visible is not None:
        w = jnp.where(visible, w, 0.0)
    return w.astype(BF16)


def _unfinished(*cs):
    m = jnp.min(cs[0])
    for c in cs[1:]:
        m = jnp.minimum(m, jnp.min(c))
    return (m < SKIP_LOG2).astype(jnp.int32)


def _more(st):
    return (st[0] >= 0) & (st[1] > 0)


NEAR_ROWS_EIGHTHS = 5


def _attn_kernel(q_ref, k_ref, v_ref, o_ref, c_ref, acc_ref, *, bq):
    i = pl.program_id(1)
    n_pairs = SB_WIDTH // LANES
    lane = lax.broadcasted_iota(jnp.int32, (bq, LANES), 1)
    row = lax.broadcasted_iota(jnp.int32, (bq, bq), 0)
    col = lax.broadcasted_iota(jnp.int32, (bq, bq), 1)
    tri = (row >= col).astype(BF16)
    diag_visible = col < row

    in_head = [(lane >= h * SB_HEAD_DIM) & (lane < (h + 1) * SB_HEAD_DIM) for h in range(HEADS_PER_BLOCK)]
    qs = []
    for p in range(n_pairs):
        qp = q_ref[:, p * LANES:(p + 1) * LANES]
        qs += [jnp.where(m, qp, jnp.zeros_like(qp)) for m in in_head]

    split = bq * NEAR_ROWS_EIGHTHS // 8
    top, bottom, whole = slice(0, split), slice(split, bq), slice(0, bq)

    def visit(j, rows, masked, first, watch):
        s0 = pl.multiple_of(j * bq, bq)
        visible = diag_visible[rows] if masked else None
        kts, vts = [], []
        for p in range(n_pairs):
            heads = slice(p * HEADS_PER_BLOCK, (p + 1) * HEADS_PER_BLOCK)
            kts.append(k_ref[heads, :, pl.ds(s0, bq)].reshape(LANES, bq).astype(BF16))
            vts.append(v_ref[heads, :, pl.ds(s0, bq)].reshape(LANES, bq).astype(BF16))
        ys = [_dot(qs[h][rows], kts[h // HEADS_PER_BLOCK]) * LOG2E for h in range(SB_HEADS)]
        sps = [_softplus2(y) if visible is None else jnp.where(visible, _softplus2(y), 0.0) for y in ys]
        cums = [_suffix_sums(sp, tri) for sp in sps]
        cs_old = [0.0 if first else c_ref[h, rows] for h in range(SB_HEADS)]
        cs = [cum[:, 0:1] + c for cum, c in zip(cums, cs_old)]
        flags = [_unfinished(*[c[w] for c in cs]) for w in watch]
        ws = [_sb_weights(y, cum, c, visible) for y, cum, c in zip(ys, cums, cs_old)]
        accs = [_dot_nt(w, vts[h // HEADS_PER_BLOCK]) for h, w in enumerate(ws)]
        for h in range(SB_HEADS):
            acc_ref[h, rows] = accs[h] if first else acc_ref[h, rows] + accs[h]
            c_ref[h, rows] = cs[h]
        return flags

    bottom_left, = visit(i, whole, True, True, [bottom])

    def previous_block():
        top_left, = visit(i - 1, top, False, False, [slice(None)])
        rest, = lax.cond(bottom_left > 0, lambda: visit(i - 1, bottom, False, False, [slice(None)]),
                         lambda: [jnp.int32(0)])
        return jnp.maximum(top_left, rest)

    go = lax.cond(i > 0, previous_block, lambda: jnp.int32(0))

    def earlier(st):
        return st[0] - 1, visit(st[0], whole, False, False, [slice(None)])[0]

    lax.while_loop(_more, earlier, (i - 2, go))
    for p in range(n_pairs):
        o_ref[:, p * LANES:(p + 1) * LANES] = jnp.where(
            in_head[0], acc_ref[p * HEADS_PER_BLOCK], acc_ref[p * HEADS_PER_BLOCK + 1]).astype(o_ref.dtype)


def _attention(q, k_t, v_t, *, bq):
    bsz, L, _ = q.shape
    qspec = pl.BlockSpec((None, bq, SB_WIDTH), lambda b, i: (b, i, 0))
    kvspec = pl.BlockSpec((None, SB_HEADS, SB_HEAD_DIM, L), lambda b, i: (b, 0, 0, 0))
    return pl.pallas_call(
        functools.partial(_attn_kernel, bq=bq),
        grid=(bsz, L // bq),
        in_specs=[qspec, kvspec, kvspec],
        out_specs=qspec,
        out_shape=jax.ShapeDtypeStruct((bsz, L, SB_WIDTH), BF16),
        scratch_shapes=[pltpu.VMEM((SB_HEADS, bq, 1), F32), pltpu.VMEM((SB_HEADS, bq, LANES), F32)],
        compiler_params=pltpu.CompilerParams(
            dimension_semantics=("parallel", "parallel"), vmem_limit_bytes=VMEM_LIMIT),
        name="sb_attention",
    )(q, k_t, v_t)


def _attn_cached_kernel(q_ref, k_ref, v_ref, kl_ref, vl_ref, kc_hbm, vc_hbm, o_ref, kbuf, vbuf, sem,
                        *, layer, lq, bk, n_past):
    b = pl.program_id(0)
    hd = SB_HEAD_DIM
    q = q_ref[...]
    kn = k_ref[...].astype(BF16)
    vn = v_ref[...].astype(BF16)
    head = lambda x, h: x[:, h * hd:(h + 1) * hd]
    qs = [head(q, h) for h in range(SB_HEADS)]
    rows = SB_HEADS * lq

    r = lax.broadcasted_iota(jnp.int32, (rows, lq), 0) % lq
    s = lax.broadcasted_iota(jnp.int32, (rows, lq), 1)
    visible = s < r
    tr = lax.broadcasted_iota(jnp.int32, (lq, lq), 0)
    tc = lax.broadcasted_iota(jnp.int32, (lq, lq), 1)
    yd = jnp.concatenate([_dot_nt(qs[h], head(kn, h)) for h in range(SB_HEADS)], axis=0) * LOG2E
    cumd = _suffix_sums(jnp.where(visible, _softplus2(yd), 0.0), (tr >= tc).astype(BF16))
    wd = _sb_weights(yd, cumd, 0.0, visible)
    accs = [_dot(wd[h * lq:(h + 1) * lq], head(vn, h)) for h in range(SB_HEADS)]
    c = cumd[:, 0:1]

    pr = lax.broadcasted_iota(jnp.int32, (bk, bk), 0)
    pc = lax.broadcasted_iota(jnp.int32, (bk, bk), 1)
    tri_p = (pr >= pc).astype(BF16)

    def cache_block(kref, vref, c, accs):
        y = jnp.concatenate([_dot(qs[h], kref[h].astype(BF16)) for h in range(SB_HEADS)], axis=0) * LOG2E
        cum = _suffix_sums(_softplus2(y), tri_p)
        w = _sb_weights(y, cum, c, None)
        accs = [accs[h] + _dot_nt(w[h * lq:(h + 1) * lq], vref[h].astype(BF16)) for h in range(SB_HEADS)]
        return c + cum[:, 0:1], accs

    c, accs = cache_block(kl_ref, vl_ref, c, accs)

    def fetch(j):
        s0 = pl.multiple_of(j * bk, bk)
        return (pltpu.make_async_copy(kc_hbm.at[layer, b, :, :, pl.ds(s0, bk)], kbuf, sem.at[0]),
                pltpu.make_async_copy(vc_hbm.at[layer, b, :, :, pl.ds(s0, bk)], vbuf, sem.at[1]))

    def visit(st):
        j, c, accs = st[0], st[2], list(st[3:])
        copies = fetch(j)
        for cp in copies:
            cp.start()
        for cp in copies:
            cp.wait()
        c, accs = cache_block(kbuf, vbuf, c, accs)
        return (j - 1, _unfinished(c), c) + tuple(accs)

    st = lax.while_loop(_more, visit, (n_past - 2, _unfinished(c), c) + tuple(accs))
    o_ref[...] = jnp.concatenate(st[3:], axis=1).astype(o_ref.dtype)


def _attention_cached(q, k, v, cache_kt, cache_vt, layer, *, bk=256):
    bsz, lq, _ = q.shape
    past = cache_kt.shape[4]
    n_past = past // bk
    tok = pl.BlockSpec((None, lq, SB_WIDTH), lambda b: (b, 0, 0))
    last = pl.BlockSpec((None, None, SB_HEADS, SB_HEAD_DIM, bk), lambda b: (layer, b, 0, 0, n_past - 1))
    hbm = pl.BlockSpec(memory_space=pl.ANY)
    return pl.pallas_call(
        functools.partial(_attn_cached_kernel, layer=layer, lq=lq, bk=bk, n_past=n_past),
        grid=(bsz,),
        in_specs=[tok, tok, tok, last, last, hbm, hbm],
        out_specs=tok,
        out_shape=jax.ShapeDtypeStruct((bsz, lq, SB_WIDTH), BF16),
        scratch_shapes=[pltpu.VMEM((SB_HEADS, SB_HEAD_DIM, bk), F32), pltpu.VMEM((SB_HEADS, SB_HEAD_DIM, bk), F32),
                        pltpu.SemaphoreType.DMA((2,))],
        compiler_params=pltpu.CompilerParams(
            dimension_semantics=("parallel",), vmem_limit_bytes=VMEM_LIMIT),
        name="sb_attention_cached",
    )(q, k, v, cache_kt, cache_vt, cache_kt, cache_vt)


def _gelu_tanh(x):
    return 0.5 * x * (1.0 + jnp.tanh(math.sqrt(2.0 / math.pi) * (x + 0.044715 * (x * x * x))))


SSM_ROW_PARTS = 4


def _ssm_kernel(u_ref, s0re_ref, s0im_ref, lbre_ref, lbim_ref, bre_ref, bim_ref, cre_ref, cimn_ref,
                d_ref, wglu_ref, o_ref, sfre_ref, sfim_ref, sre, sim, st_re, st_im, tmaj, ybuf, *, bsz, tc):
    step = pl.program_id(0)
    rows = tc * bsz
    part_rows = rows // SSM_ROW_PARTS
    part_steps = tc // SSM_ROW_PARTS

    @pl.when(step == 0)
    def _():
        st_re[...] = s0re_ref[...]
        st_im[...] = s0im_ref[...]

    for b in range(bsz):
        for m in range(SSM_SLABS):
            tmaj[m, pl.ds(b, tc, stride=bsz), :] = u_ref[b, :, m * LANES:(m + 1) * LANES]

    def part(q):
        return slice(q * part_rows, (q + 1) * part_rows)

    def slab(m):
        return slice(m * SLAB_STATE, (m + 1) * SLAB_STATE)

    def b_proj(m, q):
        um = tmaj[m, part(q), :].astype(BF16)
        sre[part(q), slab(m)] = _dot(um, bre_ref[m])
        sim[part(q), slab(m)] = _dot(um, bim_ref[m])

    def recurrence(m, q):
        lr = jnp.broadcast_to(lbre_ref[:, slab(m)], (8, SLAB_STATE))
        li = jnp.broadcast_to(lbim_ref[:, slab(m)], (8, SLAB_STATE))
        for sb in range(bsz // 8):
            subl = slice(sb * 8, (sb + 1) * 8)
            sr, si = st_re[subl, slab(m)], st_im[subl, slab(m)]
            for t in range(q * part_steps, (q + 1) * part_steps):
                r = slice(t * bsz + sb * 8, t * bsz + sb * 8 + 8)
                sr, si = (lr * sr - li * si + sre[r, slab(m)], lr * si + li * sr + sim[r, slab(m)])
                sre[r, slab(m)] = sr
                sim[r, slab(m)] = si
            st_re[subl, slab(m)] = sr
            st_im[subl, slab(m)] = si

    def c_proj(m, q):
        ybuf[m, part(q), :] = (_dot(sre[part(q), slab(m)].astype(BF16), cre_ref[m])
                               + _dot(sim[part(q), slab(m)].astype(BF16), cimn_ref[m]))

    for q in range(SSM_ROW_PARTS):
        b_proj(0, q)
    for m in range(SSM_SLABS):
        for q in range(SSM_ROW_PARTS):
            recurrence(m, q)
            if m + 1 < SSM_SLABS:
                b_proj(m + 1, q)
            c_proj(m, q)

    u = jnp.concatenate([tmaj[m] for m in range(SSM_SLABS)], axis=1)
    y = jnp.concatenate([ybuf[m] for m in range(SSM_SLABS)], axis=1) + d_ref[...] * u
    y = _gelu_tanh(y)
    o = y * _sigmoid(_dot(y.astype(BF16), wglu_ref[...]))
    for m in range(SSM_SLABS):
        tmaj[m] = o[:, m * LANES:(m + 1) * LANES]
    for b in range(bsz):
        for m in range(SSM_SLABS):
            o_ref[b, :, m * LANES:(m + 1) * LANES] = tmaj[m, pl.ds(b, tc, stride=bsz), :].astype(o_ref.dtype)

    @pl.when(step == pl.num_programs(0) - 1)
    def _():
        sfre_ref[...] = st_re[...]
        sfim_ref[...] = st_im[...]


def _ssm(u, s0_re, s0_im, prm, *, tc):
    bsz, L, _ = u.shape
    blk = tc * bsz
    row_spec = pl.BlockSpec((bsz, tc, SSM_WIDTH), lambda s: (0, s, 0))
    st_shape = (bsz, SSM_LANES)
    return pl.pallas_call(
        functools.partial(_ssm_kernel, bsz=bsz, tc=tc),
        grid=(L // tc,),
        in_specs=[row_spec, _const_spec(st_shape), _const_spec(st_shape),
                  _const_spec((1, SSM_LANES)), _const_spec((1, SSM_LANES)),
                  _const_spec((SSM_SLABS, LANES, SLAB_STATE)), _const_spec((SSM_SLABS, LANES, SLAB_STATE)),
                  _const_spec((SSM_SLABS, SLAB_STATE, LANES)), _const_spec((SSM_SLABS, SLAB_STATE, LANES)),
                  _const_spec((1, SSM_WIDTH)), _const_spec((SSM_WIDTH, SSM_WIDTH))],
        out_specs=[row_spec, _const_spec(st_shape), _const_spec(st_shape)],
        out_shape=[jax.ShapeDtypeStruct((bsz, L, SSM_WIDTH), BF16),
                   jax.ShapeDtypeStruct(st_shape, F32), jax.ShapeDtypeStruct(st_shape, F32)],
        scratch_shapes=[pltpu.VMEM((blk, SSM_LANES), F32), pltpu.VMEM((blk, SSM_LANES), F32),
                        pltpu.VMEM(st_shape, F32), pltpu.VMEM(st_shape, F32),
                        pltpu.VMEM((SSM_SLABS, blk, LANES), F32),
                        pltpu.VMEM((SSM_SLABS, blk, LANES), F32)],
        compiler_params=pltpu.CompilerParams(
            dimension_semantics=("arbitrary",), vmem_limit_bytes=VMEM_LIMIT),
        name="s5_scan",
    )(u, s0_re, s0_im, prm["lb_re"], prm["lb_im"], prm["bd_re"], prm["bd_im"],
      prm["cd_re"], prm["cd_im_neg"], prm["d"], prm["w_glu"])


def _ssm_params(a_re, a_im, log_dt, b_re, b_im, c_re, c_im, d, w_glu):
    dt = jnp.exp(log_dt)[:, None]
    mag = jnp.exp(a_re * dt)
    lb_re = mag * jnp.cos(a_im * dt)
    lb_im = mag * jnp.sin(a_im * dt)
    den = a_re * a_re + a_im * a_im
    nr, ni = lb_re - 1.0, lb_im
    f_re = (nr * a_re + ni * a_im) / den
    f_im = (ni * a_re - nr * a_im) / den
    bb_re = f_re[:, :, None] * b_re - f_im[:, :, None] * b_im
    bb_im = f_re[:, :, None] * b_im + f_im[:, :, None] * b_re
    gps = SSM_GROUPS // SSM_SLABS
    eye = jnp.eye(gps, dtype=F32)

    def b_slabs(bb):
        t = bb.transpose(0, 2, 1).reshape(SSM_SLABS, gps, SSM_GROUP, SSM_STATE)
        return jnp.einsum("mgcp,gh->mgchp", t, eye).reshape(SSM_SLABS, LANES, SLAB_STATE).astype(BF16)

    def c_slabs(c):
        t = c.transpose(0, 2, 1).reshape(SSM_SLABS, gps, SSM_STATE, SSM_GROUP)
        return jnp.einsum("mgpc,gh->mgphc", t, eye).reshape(SSM_SLABS, SLAB_STATE, LANES).astype(BF16)

    return dict(lb_re=lb_re.reshape(1, SSM_LANES), lb_im=lb_im.reshape(1, SSM_LANES),
                bd_re=b_slabs(bb_re), bd_im=b_slabs(bb_im),
                cd_re=c_slabs(c_re), cd_im_neg=c_slabs(-c_im),
                d=d.reshape(1, SSM_WIDTH), w_glu=w_glu.astype(BF16))


def _post_kernel(x_ref, oa_ref, os_ref, ga_ref, gs_ref, p_ref,
                 wba_ref, wbs_ref, wout_ref, wg_ref, wu_ref, wd_ref, wpg_ref, wpp_ref,
                 n_mix_post, n_ffn_pre, n_ffn_post, n_ple_pre, n_ple_post,
                 y_ref, *, nb, tm, ff_chunk):
    rows = nb * tm
    x = x_ref[...].reshape(rows, D_MODEL)
    oa = oa_ref[...].reshape(rows, SB_WIDTH)
    os_ = os_ref[...].reshape(rows, SSM_WIDTH)
    merged = (_sigmoid(ga_ref[...].reshape(rows, D_MODEL)) * _dot(oa, wba_ref[...])
              + _sigmoid(gs_ref[...].reshape(rows, D_MODEL)) * _dot(os_, wbs_ref[...]))
    x = x + _rms(_dot(merged.astype(BF16), wout_ref[...]), n_mix_post[...])

    f = _rms(x, n_ffn_pre[...]).astype(BF16)
    ff = jnp.zeros((rows, D_MODEL), F32)
    for c in range(D_FF // ff_chunk):
        cols = slice(c * ff_chunk, (c + 1) * ff_chunk)
        g = _dot(f, wg_ref[:, cols])
        a = (g * _sigmoid(g)) * _dot(f, wu_ref[:, cols])
        ff = ff + _dot(a.astype(BF16), wd_ref[cols, :])
    x = x + _rms(ff, n_ffn_post[...])

    gate = _sigmoid(_dot(_rms(x, n_ple_pre[...]).astype(BF16), wpg_ref[...]))
    pe = gate * _dot(p_ref[...].reshape(rows, PLE_DIM).astype(BF16), wpp_ref[...])
    y_ref[...] = (x + _rms(pe, n_ple_post[...])).reshape(nb, tm, D_MODEL)


def _post(x, o_attn, o_ssm, g_attn, g_ssm, p, W, *, nb, tm, ff_chunk=256):
    bsz, L, _ = x.shape
    grid = (bsz // nb, L // tm)
    tok = lambda width: pl.BlockSpec((nb, tm, width), lambda b, i: (b, i, 0))
    vec = _const_spec((1, D_MODEL))
    return pl.pallas_call(
        functools.partial(_post_kernel, nb=nb, tm=tm, ff_chunk=ff_chunk),
        grid=grid,
        in_specs=[tok(D_MODEL), tok(SB_WIDTH), tok(SSM_WIDTH),
                  tok(D_MODEL), tok(D_MODEL), tok(PLE_DIM),
                  _const_spec((SB_WIDTH, D_MODEL)), _const_spec((SSM_WIDTH, D_MODEL)),
                  _const_spec((D_MODEL, D_MODEL)),
                  _const_spec((D_MODEL, D_FF)), _const_spec((D_MODEL, D_FF)), _const_spec((D_FF, D_MODEL)),
                  _const_spec((D_MODEL, D_MODEL)), _const_spec((PLE_DIM, D_MODEL)),
                  vec, vec, vec, vec, vec],
        out_specs=tok(D_MODEL),
        out_shape=jax.ShapeDtypeStruct((bsz, L, D_MODEL), F32),
        compiler_params=pltpu.CompilerParams(
            dimension_semantics=("parallel", "parallel"), vmem_limit_bytes=VMEM_LIMIT),
        name="post",
    )(x, o_attn, o_ssm, g_attn, g_ssm, p,
      W["w_branch_attn"], W["w_branch_ssm"], W["w_out"], W["w_ffn_gate"], W["w_ffn_up"], W["w_ffn_down"],
      W["w_ple_gate"], W["w_ple_proj"],
      W["norm_mix_post"], W["norm_ffn_pre"], W["norm_ffn_post"], W["norm_ple_pre"], W["norm_ple_post"])


ROW_TILE = 512
QUERY_BLOCK = 256
INPROJ_LONG_TILE = 1024


def _layer(x, p, cache, s_re0, s_im0, W):
    bsz, L, _ = x.shape
    tm = min(ROW_TILE, L)
    nb = ROW_TILE // tm
    bq = min(QUERY_BLOCK, L)
    tc = ROW_TILE // bsz
    in_nb, in_tm = (1, INPROJ_LONG_TILE) if L % INPROJ_LONG_TILE == 0 else (nb, tm)
    q, k, v, u, g_attn, g_ssm = _inproj(x, W["norm_mix_pre"], W["w_in"], W["w_kv_t"], nb=in_nb, tm=in_tm,
                                           kv_t=cache is None)
    if cache is None:
        o_attn = _attention(q, k, v, bq=bq)
        k, v = (jnp.transpose(a, (0, 3, 1, 2)) for a in (k, v))
    else:
        o_attn = _attention_cached(q, k, v, *cache)
    o_ssm, s_re, s_im = _ssm(u, s_re0.reshape(bsz, SSM_LANES), s_im0.reshape(bsz, SSM_LANES), W["ssm"], tc=tc)
    y = _post(x, o_attn, o_ssm, g_attn, g_ssm, p, W, nb=nb, tm=tm)
    heads = (bsz, L, SB_HEADS, SB_HEAD_DIM)
    state = (bsz, SSM_GROUPS, SSM_STATE)
    return y, k.reshape(heads), v.reshape(heads), s_re.reshape(state), s_im.reshape(state)


def kernel(x_prompt, x_sample, cache_k, cache_v, state_ssm_re, state_ssm_im, p_prompt, p_sample, norm_mix_pre, norm_mix_post, w_in, ssm_a_re, ssm_a_im, ssm_log_dt, ssm_b_re, ssm_b_im, ssm_c_re, ssm_c_im, ssm_d, w_glu, w_branch_attn, w_branch_ssm, w_out, norm_ffn_pre, norm_ffn_post, w_ffn_gate, w_ffn_up, w_ffn_down, norm_ple_pre, norm_ple_post, w_ple_gate, w_ple_proj):
    depth = w_in.shape[0]
    yp, ys = x_prompt, x_sample
    outs = [[] for _ in range(8)]
    cache_kt, cache_vt = (jnp.transpose(c, (0, 1, 3, 4, 2)) for c in (cache_k, cache_v))
    for i in range(depth):
        W = dict(
            norm_mix_pre=norm_mix_pre[i][None], norm_mix_post=norm_mix_post[i][None],
            norm_ffn_pre=norm_ffn_pre[i][None], norm_ffn_post=norm_ffn_post[i][None],
            norm_ple_pre=norm_ple_pre[i][None], norm_ple_post=norm_ple_post[i][None],
            w_in=w_in[i].astype(BF16), w_kv_t=w_in[i][:, SB_WIDTH:3 * SB_WIDTH].T.astype(BF16),
            w_branch_attn=w_branch_attn[i].astype(BF16),
            w_branch_ssm=w_branch_ssm[i].astype(BF16), w_out=w_out[i].astype(BF16),
            w_ffn_gate=w_ffn_gate[i].astype(BF16), w_ffn_up=w_ffn_up[i].astype(BF16),
            w_ffn_down=w_ffn_down[i].astype(BF16), w_ple_gate=w_ple_gate[i].astype(BF16),
            w_ple_proj=w_ple_proj[i].astype(BF16),
            ssm=_ssm_params(ssm_a_re[i], ssm_a_im[i], ssm_log_dt[i], ssm_b_re[i], ssm_b_im[i],
                            ssm_c_re[i], ssm_c_im[i], ssm_d[i], w_glu[i]))
        zero_state = jnp.zeros((yp.shape[0], SSM_GROUPS, SSM_STATE), F32)
        yp, kp, vp, srp, sip = _layer(yp, p_prompt[i], None, zero_state, zero_state, W)
        ys, kn, vn, srs, sis = _layer(
            ys, p_sample[i], (cache_kt, cache_vt, i), state_ssm_re[i], state_ssm_im[i], W)
        for lst, val in zip(outs, (kp, vp, srp, sip, kn, vn, srs, sis)):
            lst.append(val)
    return (yp, ys) + tuple(jnp.stack(o) for o in outs)
```
